```python
import jax
import jax.numpy as jnp
from jax import lax
import numpy as np

D_MODEL = 1024
BATCH = 2
SEQ = 8192
DEPTH = 1
DEC_BATCH = 4
DEC_SEQ = 4096
PAST_LEN = 128

LA_HEADS = 8
LA_DK = 128
LA_DV = 128
CONV_W = 4
CONV_PAD = (2, 1)
CHUNK = 64
ATT_Q_HEADS = 8
ATT_KV_HEADS = 2
ATT_HD = 128
WINDOW = 128
ATT_BLOCK = 128
ROPE_THETA = 10000.0
N_EXPERTS = 16
EXPERT_FF = 2048
CAPACITY_FACTOR = 2
EPS = 1e-6

LA_QK = LA_HEADS * LA_DK
LA_V = LA_HEADS * LA_DV
ATT_Q = ATT_Q_HEADS * ATT_HD
ATT_KV = ATT_KV_HEADS * ATT_HD
CONV_CH = 2 * LA_QK + LA_V
PROJ_SIZES = (CONV_CH, LA_V, 2 * LA_HEADS, 2 * LA_HEADS, ATT_Q, ATT_KV, ATT_KV, D_MODEL, D_MODEL)
PROJ_COLS = CONV_CH + LA_V + 4 * LA_HEADS + ATT_Q + 2 * ATT_KV + 2 * D_MODEL

kernel_name = 'hybrid_deltanet_swa_ec_moe_encoder'


def rms_norm(x, g):
    xf = x.astype(jnp.float32)
    y = xf * lax.rsqrt(jnp.mean(xf * xf, axis=-1, keepdims=True) + EPS)
    return (y * g.astype(jnp.float32)).astype(x.dtype)


def l2_normalize(x):
    return x * lax.rsqrt(jnp.sum(x * x, axis=-1, keepdims=True) + EPS)


def short_conv(x, w):
    y = lax.conv_general_dilated(x, w[:, None, :].astype(x.dtype), window_strides=(1,), padding=[CONV_PAD], dimension_numbers=('NWC', 'WIO', 'NWC'), feature_group_count=x.shape[-1])
    return jax.nn.silu(y)


def rope(x):
    s, dh = x.shape[1], x.shape[-1]
    half = dh // 2
    inv_freq = ROPE_THETA ** (-jnp.arange(half, dtype=jnp.float32) / half)
    ang = jnp.arange(s, dtype=jnp.float32)[:, None] * inv_freq[None, :]
    cos = jnp.cos(ang)[None, :, None, :]
    sin = jnp.sin(ang)[None, :, None, :]
    x1, x2 = x[..., :half], x[..., half:]
    return jnp.concatenate([x1 * cos - x2 * sin, x2 * cos + x1 * sin], axis=-1)


def gated_delta_chunked(q, k, v, g, beta):
    B, S, H, DK = q.shape
    DV = v.shape[-1]
    N = S // CHUNK
    def chunks(t):
        return t.reshape(B, N, CHUNK, H, t.shape[-1]).transpose(0, 1, 3, 2, 4)
    qc, kc, vc = chunks(q), chunks(k), chunks(v)
    gc = jnp.cumsum(g.reshape(B, N, CHUNK, H).transpose(0, 1, 3, 2), axis=-1)
    bc = beta.reshape(B, N, CHUNK, H).transpose(0, 1, 3, 2)
    idx = jnp.arange(CHUNK)
    lower = idx[:, None] >= idx[None, :]
    strict = idx[:, None] > idx[None, :]
    decay = jnp.exp(jnp.where(lower, gc[..., :, None] - gc[..., None, :], -jnp.inf))
    kb = kc * bc[..., None]
    L = jnp.where(strict, jnp.einsum('bnhid,bnhjd->bnhij', kb, kc) * decay, 0.0)
    eye = jnp.eye(CHUNK, dtype=jnp.float32)
    T = lax.linalg.triangular_solve(eye + L, jnp.broadcast_to(eye, L.shape), left_side=True, lower=True, unit_diagonal=True)
    u = jnp.einsum('bnhij,bnhje->bnhie', T, vc * bc[..., None])
    w = jnp.einsum('bnhij,bnhjd->bnhid', T, kb * jnp.exp(gc)[..., None])
    qk = jnp.einsum('bnhid,bnhjd->bnhij', qc, kc) * decay

    def step(state, inp):
        q_i, k_i, u_i, w_i, g_i, qk_i = inp
        v_new = u_i - jnp.einsum('bhcd,bhde->bhce', w_i, state)
        o_i = jnp.einsum('bhcd,bhde->bhce', q_i * jnp.exp(g_i)[..., None], state) + jnp.einsum('bhij,bhje->bhie', qk_i, v_new)
        g_last = g_i[..., -1]
        k_dec = k_i * jnp.exp(g_last[..., None] - g_i)[..., None]
        state = state * jnp.exp(g_last)[..., None, None] + jnp.einsum('bhcd,bhce->bhde', k_dec, v_new)
        return state, o_i

    xs = tuple(jnp.moveaxis(t, 1, 0) for t in (qc, kc, u, w, gc, qk))
    s0 = jnp.zeros((B, H, DK, DV), jnp.float32)
    _, o = lax.scan(step, s0, xs)
    return o.transpose(1, 0, 3, 2, 4).reshape(B, S, H, DV)


def deltanet_branch(qkv_raw, z, beta_raw, alpha_raw, conv_w, a_log, dt_bias, norm_g):
    B, S, _ = qkv_raw.shape
    qkv = short_conv(qkv_raw, conv_w).astype(jnp.float32)
    q = l2_normalize(qkv[..., :LA_QK].reshape(B, S, LA_HEADS, LA_DK)) * (LA_DK ** -0.5)
    k = l2_normalize(qkv[..., LA_QK:2 * LA_QK].reshape(B, S, LA_HEADS, LA_DK))
    v = qkv[..., 2 * LA_QK:].reshape(B, S, LA_HEADS, LA_DV)
    beta = jax.nn.sigmoid(beta_raw.astype(jnp.float32).reshape(B, S, 2, LA_HEADS))
    alpha = alpha_raw.astype(jnp.float32).reshape(B, S, 2, LA_HEADS)
    g = -jnp.exp(a_log.astype(jnp.float32)) * jax.nn.softplus(alpha + dt_bias.astype(jnp.float32))
    o_fwd = gated_delta_chunked(q, k, v, g[:, :, 0], beta[:, :, 0])
    flip = lambda t: jnp.flip(t, axis=1)
    o_bwd = flip(gated_delta_chunked(flip(q), flip(k), flip(v), flip(g[:, :, 1]), flip(beta[:, :, 1])))
    o = o_fwd + o_bwd
    o = o * lax.rsqrt(jnp.mean(o * o, axis=-1, keepdims=True) + EPS) * norm_g.astype(jnp.float32)
    o = o * jax.nn.silu(z.astype(jnp.float32).reshape(B, S, LA_HEADS, LA_DV))
    return o.reshape(B, S, LA_V)


def window_attention(q_raw, k_raw, v_raw, sink):
    B, S, _ = q_raw.shape
    NB = S // ATT_BLOCK
    G = ATT_Q_HEADS // ATT_KV_HEADS
    q = rope(q_raw.astype(jnp.float32).reshape(B, S, ATT_Q_HEADS, ATT_HD))
    k = rope(k_raw.astype(jnp.float32).reshape(B, S, ATT_KV_HEADS, ATT_HD))
    v = v_raw.astype(jnp.float32).reshape(B, S, ATT_KV_HEADS, ATT_HD)
    qb = q.reshape(B, NB, ATT_BLOCK, ATT_KV_HEADS, G, ATT_HD)
    def band(t):
        tp = jnp.pad(t, ((0, 0), (ATT_BLOCK, ATT_BLOCK), (0, 0), (0, 0)))
        tb = tp.reshape(B, NB + 2, ATT_BLOCK, ATT_KV_HEADS, ATT_HD)
        return jnp.concatenate([tb[:, :-2], tb[:, 1:-1], tb[:, 2:]], axis=2)
    kw, vw = band(k), band(v)
    s = jnp.einsum('bnqhgd,bnkhd->bnhgqk', qb, kw) * (ATT_HD ** -0.5)
    blk = jnp.arange(NB)[:, None, None]
    qpos = blk * ATT_BLOCK + jnp.arange(ATT_BLOCK)[None, :, None]
    kpos = (blk - 1) * ATT_BLOCK + jnp.arange(3 * ATT_BLOCK)[None, None, :]
    valid = (jnp.abs(qpos - kpos) <= WINDOW) & (kpos >= 0) & (kpos < S)
    s = jnp.where(valid[None, :, None, None], s, -jnp.inf)
    sk = sink.astype(jnp.float32).reshape(ATT_KV_HEADS, G)[None, None, :, :, None, None]
    m = jnp.maximum(jnp.max(s, axis=-1, keepdims=True), sk)
    p = jnp.exp(s - m)
    p = p / (jnp.sum(p, axis=-1, keepdims=True) + jnp.exp(sk - m))
    o = jnp.einsum('bnhgqk,bnkhd->bnqhgd', p, vw)
    return o.reshape(B, S, ATT_Q)


def token_mixing(h, w_in, conv_w, a_log, dt_bias, la_norm_g, attn_sink, w_branch_a, w_branch_b, w_out):
    proj = h @ w_in
    split_at = np.cumsum(PROJ_SIZES)[:-1]
    qkv_a, z_a, beta_a, alpha_a, q_b, k_b, v_b, gate_a, gate_b = jnp.split(proj, split_at, axis=-1)
    o_a = deltanet_branch(qkv_a, z_a, beta_a, alpha_a, conv_w, a_log, dt_bias, la_norm_g).astype(h.dtype)
    o_b = window_attention(q_b, k_b, v_b, attn_sink).astype(h.dtype)
    merged = jax.nn.sigmoid(gate_a) * (o_a @ w_branch_a) + jax.nn.sigmoid(gate_b) * (o_b @ w_branch_b)
    return merged @ w_out


def expert_choice_ffn(h, w_router, w_gate, w_up, w_down):
    B, S, D = h.shape
    T = B * S
    C = max(1, CAPACITY_FACTOR * T // N_EXPERTS)
    ht = h.reshape(T, D)
    aff = jax.nn.softmax((ht @ w_router).astype(jnp.float32), axis=-1)
    top_aff, top_idx = lax.top_k(aff.T, C)
    xe = ht[top_idx]
    a = jnp.einsum('ecd,edf->ecf', xe, w_gate)
    b = jnp.einsum('ecd,edf->ecf', xe, w_up)
    ye = jnp.einsum('ecf,efd->ecd', jax.nn.silu(a) * b, w_down)
    ye = ye * top_aff[..., None].astype(ye.dtype)
    out = jnp.zeros((T, D), ye.dtype).at[top_idx.reshape(-1)].add(ye.reshape(-1, D))
    return out.reshape(B, S, D).astype(h.dtype)


def setup_inputs(seed: int = 0) -> dict:
    key = jax.random.key(seed)
    ks = jax.random.split(key, 18)
    f32 = jnp.float32
    def dense(k, shape, fan_in):
        return jax.random.normal(k, shape, f32) * (fan_in ** -0.5)
    def gain(k, shape):
        return 1.0 + 0.02 * jax.random.normal(k, shape, f32)
    x_prompt = jax.random.normal(ks[0], (BATCH, SEQ, D_MODEL), f32)
    x_sample = jax.random.normal(ks[1], (DEC_BATCH, DEC_SEQ, D_MODEL), f32)
    norm_mix_g = gain(ks[2], (DEPTH, D_MODEL))
    w_in = dense(ks[3], (DEPTH, D_MODEL, PROJ_COLS), D_MODEL)
    conv_w = dense(ks[4], (DEPTH, CONV_W, CONV_CH), CONV_W)
    la_a_log = jnp.log(jax.random.uniform(ks[5], (DEPTH, 2, LA_HEADS), f32, 1.0, 16.0))
    dt = jnp.exp(jax.random.uniform(ks[6], (DEPTH, 2, LA_HEADS), f32, float(np.log(1e-3)), float(np.log(1e-1))))
    la_dt_bias = dt + jnp.log(-jnp.expm1(-dt))
    la_norm_g = gain(ks[7], (DEPTH, LA_DV))
    attn_sink = 0.5 * jax.random.normal(ks[8], (DEPTH, ATT_Q_HEADS), f32)
    w_branch_a = dense(ks[9], (DEPTH, LA_V, D_MODEL), LA_V)
    w_branch_b = dense(ks[10], (DEPTH, ATT_Q, D_MODEL), ATT_Q)
    w_out = dense(ks[11], (DEPTH, D_MODEL, D_MODEL), D_MODEL)
    norm_ffn_g = gain(ks[12], (DEPTH, D_MODEL))
    w_router = dense(ks[13], (DEPTH, D_MODEL, N_EXPERTS), D_MODEL)
    w_exp_gate = dense(ks[14], (DEPTH, N_EXPERTS, D_MODEL, EXPERT_FF), D_MODEL)
    w_exp_up = dense(ks[15], (DEPTH, N_EXPERTS, D_MODEL, EXPERT_FF), D_MODEL)
    w_exp_down = dense(ks[16], (DEPTH, N_EXPERTS, EXPERT_FF, D_MODEL), EXPERT_FF)
    final_norm_g = gain(ks[17], (D_MODEL,))
    return {'x_prompt': x_prompt, 'x_sample': x_sample, 'norm_mix_g': norm_mix_g, 'w_in': w_in, 'conv_w': conv_w, 'la_a_log': la_a_log, 'la_dt_bias': la_dt_bias, 'la_norm_g': la_norm_g, 'attn_sink': attn_sink, 'w_branch_a': w_branch_a, 'w_branch_b': w_branch_b, 'w_out': w_out, 'norm_ffn_g': norm_ffn_g, 'w_router': w_router, 'w_exp_gate': w_exp_gate, 'w_exp_up': w_exp_up, 'w_exp_down': w_exp_down, 'final_norm_g': final_norm_g}


def reference(x_prompt, x_sample, norm_mix_g, w_in, conv_w, la_a_log, la_dt_bias, la_norm_g, attn_sink, w_branch_a, w_branch_b, w_out, norm_ffn_g, w_router, w_exp_gate, w_exp_up, w_exp_down, final_norm_g):
    def trunk(x):
        for l in range(DEPTH):
            x = x + token_mixing(rms_norm(x, norm_mix_g[l]), w_in[l], conv_w[l], la_a_log[l], la_dt_bias[l], la_norm_g[l], attn_sink[l], w_branch_a[l], w_branch_b[l], w_out[l])
            x = x + expert_choice_ffn(rms_norm(x, norm_ffn_g[l]), w_router[l], w_exp_gate[l], w_exp_up[l], w_exp_down[l])
        return rms_norm(x, final_norm_g)
    y_prompt = trunk(x_prompt)
    y_sample = trunk(x_sample)
    return (y_prompt, y_sample)
```

```python
import functools

import jax
import jax.numpy as jnp
from jax import lax
from jax.experimental import pallas as pl
from jax.experimental.pallas import tpu as pltpu

D_MODEL = 1024
LA_HEADS = 8
LA_DK = 128
LA_DV = 128
CONV_W = 4
ATT_Q_HEADS = 8
ATT_KV_HEADS = 2
ATT_HD = 128
WINDOW = 128
ROPE_THETA = 10000.0
N_EXPERTS = 16
EXPERT_FF = 2048
CAPACITY_FACTOR = 2
EPS = 1e-6

LA_QK = LA_HEADS * LA_DK
LA_V = LA_HEADS * LA_DV
ATT_Q = ATT_Q_HEADS * ATT_HD
ATT_KV = ATT_KV_HEADS * ATT_HD
CONV_CH = 2 * LA_QK + LA_V
ATT_COLS = ATT_Q + 2 * ATT_KV

LANES = 128
SUBLANES = 8
BF16_ROWS = 16
VMEM_LIMIT = 56 * 1024 * 1024

DELTA_CHUNK = 128
ATT_BLOCK = 128
PROJ_TM = 512
PREP_TM = 256
MERGE_TM = 256
DISPATCH_TT = 256
DISPATCH_SUB = 4
DISPATCH_W = DISPATCH_TT + BF16_ROWS
COMBINE_TT = 128
COMBINE_W = 256
FFN_TM = 512
SELECT_BLK = 128

_BF16 = jnp.bfloat16
_F32 = jnp.float32


def _params(*sem):
    return pltpu.CompilerParams(dimension_semantics=sem, vmem_limit_bytes=VMEM_LIMIT)


def _rms(x, g):
    return x * lax.rsqrt(jnp.mean(x * x, axis=-1, keepdims=True) + EPS) * g


def _sigmoid(x):
    return 1.0 / (1.0 + jnp.exp(-x))


def _mm(a, b):
    return jnp.dot(a, b, preferred_element_type=_F32)


def _mm_nt(a, b):
    return lax.dot_general(a, b, (((1,), (1,)), ((), ())), preferred_element_type=_F32)


def _mm_tn(a, b):
    return lax.dot_general(a, b, (((0,), (0,)), ((), ())), preferred_element_type=_F32)


def _split3(x):
    h = x.astype(_BF16)
    r = x - h.astype(_F32)
    m = r.astype(_BF16)
    l = (r - m.astype(_F32)).astype(_BF16)
    return h, m, l


def _mm_exact_lhs(a01, x):
    h, m, l = _split3(x)
    return _mm(a01, h) + _mm(a01, m) + _mm(a01, l)


def _proj_body(x_ref, g_ref, w_ref, *refs, epilogue, n_extra):
    extra, o_ref, hn_ref = refs[:n_extra], refs[n_extra], refs[n_extra + 1]

    @pl.when(pl.program_id(1) == 0)
    def _():
        hn_ref[...] = _rms(x_ref[...], g_ref[...]).astype(_BF16)

    epilogue(_mm(hn_ref[...], w_ref[...]), o_ref, *extra)


def _ep_store(acc, o_ref):
    o_ref[...] = acc.astype(o_ref.dtype)


def _ep_sigmoid(acc, o_ref):
    o_ref[...] = _sigmoid(acc).astype(o_ref.dtype)


def _ep_beta_decay(acc, o_ref, a_ref, dt_ref):
    z = acc + dt_ref[...]
    sp = jnp.maximum(z, 0.0) + jnp.log1p(jnp.exp(-jnp.abs(z)))
    lane = lax.broadcasted_iota(jnp.int32, acc.shape, 1)
    g = jnp.where(lane < 4 * LA_HEADS, -jnp.exp(a_ref[...]) * sp, 0.0)
    o_ref[...] = jnp.where(lane < 2 * LA_HEADS, _sigmoid(acc), g)


def _ep_rope(acc, o_ref, cos_ref, sin_ref):
    c, s = cos_ref[...], sin_ref[...]
    for h in range(ATT_Q_HEADS + ATT_KV_HEADS):
        sl = slice(h * ATT_HD, (h + 1) * ATT_HD)
        x = acc[:, sl]
        o_ref[:, sl] = (x * c + pltpu.roll(x, ATT_HD // 2, 1) * s).astype(o_ref.dtype)
    o_ref[:, ATT_Q + ATT_KV:] = acc[:, ATT_Q + ATT_KV:].astype(o_ref.dtype)


def _project(x, g, w, out_dtype, epilogue, tn, extras=(), extra_specs=()):
    t, d = x.shape
    n = w.shape[1]
    tm = min(PROJ_TM, t)
    body = functools.partial(_proj_body, epilogue=epilogue, n_extra=len(extras))
    return pl.pallas_call(
        body,
        grid=(t // tm, n // tn),
        in_specs=[pl.BlockSpec((tm, d), lambda i, j: (i, 0)),
                  pl.BlockSpec((1, d), lambda i, j: (0, 0)),
                  pl.BlockSpec((d, tn), lambda i, j: (0, j)),
                  *extra_specs],
        out_specs=pl.BlockSpec((tm, tn), lambda i, j: (i, j)),
        out_shape=jax.ShapeDtypeStruct((t, n), out_dtype),
        scratch_shapes=[pltpu.VMEM((tm, d), _BF16)],
        compiler_params=_params("parallel", "arbitrary"),
        name="in_proj",
    )(x, g, w, *extras)


def _prep_body(cur_ref, prev_ref, next_ref, w_ref, q_ref, k_ref, v_ref, ext_ref, *, tiles_per_seq, tm):
    i = pl.program_id(0)
    pos = lax.rem(i, tiles_per_seq)
    keep_prev = jnp.where(pos == 0, 0.0, 1.0)
    keep_next = jnp.where(pos == tiles_per_seq - 1, 0.0, 1.0)
    for c, o_ref in enumerate((q_ref, k_ref, v_ref)):
        sl = slice(c * LA_QK, (c + 1) * LA_QK)
        ext_ref[0:SUBLANES, :] = prev_ref[:, sl] * keep_prev
        ext_ref[SUBLANES:SUBLANES + tm, :] = cur_ref[:, sl]
        ext_ref[SUBLANES + tm:, :] = next_ref[:, sl] * keep_next
        w = w_ref[:, sl]
        y = (ext_ref[SUBLANES - 2:SUBLANES - 2 + tm, :] * w[0:1]
             + ext_ref[SUBLANES - 1:SUBLANES - 1 + tm, :] * w[1:2]
             + ext_ref[SUBLANES:SUBLANES + tm, :] * w[2:3]
             + ext_ref[SUBLANES + 1:SUBLANES + 1 + tm, :] * w[3:4])
        y = y * _sigmoid(y)
        if c == 2:
            o_ref[...] = y.astype(o_ref.dtype)
            continue
        scale = LA_DK ** -0.5 if c == 0 else 1.0
        for h in range(LA_HEADS):
            hs = slice(h * LA_DK, (h + 1) * LA_DK)
            yh = y[:, hs]
            yh = yh * lax.rsqrt(jnp.sum(yh * yh, axis=-1, keepdims=True) + EPS)
            o_ref[:, hs] = (yh * scale).astype(o_ref.dtype)


def _delta_prep(qkv_raw, conv_w, seq):
    t = qkv_raw.shape[0]
    tm = min(PREP_TM, seq)
    halo_blocks = tm // SUBLANES
    n_halo = t // SUBLANES
    body = functools.partial(_prep_body, tiles_per_seq=seq // tm, tm=tm)
    out = jax.ShapeDtypeStruct((t, LA_QK), _BF16)
    return pl.pallas_call(
        body,
        grid=(t // tm,),
        in_specs=[pl.BlockSpec((tm, CONV_CH), lambda i: (i, 0)),
                  pl.BlockSpec((SUBLANES, CONV_CH), lambda i: (jnp.maximum(i * halo_blocks - 1, 0), 0)),
                  pl.BlockSpec((SUBLANES, CONV_CH), lambda i: (jnp.minimum((i + 1) * halo_blocks, n_halo - 1), 0)),
                  pl.BlockSpec((CONV_W, CONV_CH), lambda i: (0, 0))],
        out_specs=[pl.BlockSpec((tm, LA_QK), lambda i: (i, 0))] * 3,
        out_shape=[out, out, out],
        scratch_shapes=[pltpu.VMEM((tm + 2 * SUBLANES, LA_QK), _F32)],
        compiler_params=_params("parallel"),
        name="delta_prep",
    )(qkv_raw, qkv_raw, qkv_raw, conv_w)


def _delta_direction(d, q_ref, k_ref, v_ref, bg_ref, gt_ref, o_ref, state_ref):
    c = DELTA_CHUNK
    row = lax.broadcasted_iota(jnp.int32, (c, c), 0)
    col = lax.broadcasted_iota(jnp.int32, (c, c), 1)
    incl = (row >= col) if d == 0 else (row <= col)
    strict = (row > col) if d == 0 else (row < col)
    eye = jnp.where(row == col, 1.0, 0.0)
    incl_b = jnp.where(incl, 1.0, 0.0).astype(_BF16)
    incl_t = jnp.where((row <= col) if d == 0 else (row >= col), 1.0, 0.0).astype(_BF16)

    bg = bg_ref[...]
    gc_cols = _mm_exact_lhs(incl_b, bg)
    g_rows = gt_ref[0]
    h3 = _split3(g_rows)
    gc_rows = _mm(h3[0], incl_t) + _mm(h3[1], incl_t) + _mm(h3[2], incl_t)
    total_row = c - 1 if d == 0 else 0

    for h in range(LA_HEADS):
        hs = slice(h * LA_DK, (h + 1) * LA_DK)
        bcol = LA_HEADS * d + h
        gcol = 2 * LA_HEADS + LA_HEADS * d + h
        beta = bg[:, bcol:bcol + 1]
        gc = gc_cols[:, gcol:gcol + 1]
        gc_r = gc_rows[gcol:gcol + 1, :]
        g_tot = gc_cols[total_row:total_row + 1, gcol:gcol + 1]

        q = q_ref[:, hs].astype(_F32)
        k = k_ref[:, hs]
        kf = k.astype(_F32)
        v = v_ref[:, hs].astype(_F32)

        diff = gc - gc_r
        decay = jnp.where(incl, jnp.exp(jnp.where(incl, diff, 0.0)), 0.0)
        kb = kf * beta
        egc = jnp.exp(gc)
        a_qk = _mm_nt(jnp.concatenate([kb, q], axis=0).astype(_BF16), k)
        l_mat = jnp.where(strict, a_qk[:c] * decay, 0.0)
        qk = a_qk[c:] * decay

        x = eye - l_mat
        m_b = l_mat.astype(_BF16)
        p = 1
        while 2 * p < c:
            m = _mm(m_b, m_b)
            m_b = m.astype(_BF16)
            x = x + _mm(x.astype(_BF16), m_b)
            p *= 2
        t_b = x.astype(_BF16)

        uw = _mm(t_b, jnp.concatenate([v * beta, kb * egc], axis=1).astype(_BF16))
        u, w = uw[:, :LA_DV], uw[:, LA_DV:]
        state = state_ref[d, h]
        ws_qs = _mm(jnp.concatenate([w, q * egc], axis=0).astype(_BF16), state.astype(_BF16))
        v_new = u - ws_qs[:c]
        v_new_b = v_new.astype(_BF16)
        o_ref[:, hs] = ws_qs[c:] + _mm(qk.astype(_BF16), v_new_b)
        k_dec = (kf * jnp.exp(g_tot - gc)).astype(_BF16)
        state_ref[d, h] = state * jnp.exp(g_tot) + _mm_tn(k_dec, v_new_b)


def _delta_body(qf, kf, vf, bgf, gtf, qb, kb, vb, bgb, gtb, of_ref, ob_ref, state_ref):
    @pl.when(pl.program_id(1) == 0)
    def _():
        state_ref[...] = jnp.zeros_like(state_ref)

    _delta_direction(0, qf, kf, vf, bgf, gtf, of_ref, state_ref)
    _delta_direction(1, qb, kb, vb, bgb, gtb, ob_ref, state_ref)


def _delta_rule(q, k, v, bg, g_t, batch, seq):
    t = q.shape[0]
    c = DELTA_CHUNK
    n = seq // c
    fwd = lambda b, j: (b * n + j, 0)
    bwd = lambda b, j: (b * n + n - 1 - j, 0)
    fwd3 = lambda b, j: (b * n + j, 0, 0)
    bwd3 = lambda b, j: (b * n + n - 1 - j, 0, 0)

    def specs(im, im3):
        return [pl.BlockSpec((c, LA_QK), im), pl.BlockSpec((c, LA_QK), im), pl.BlockSpec((c, LA_V), im),
                pl.BlockSpec((c, LANES), im), pl.BlockSpec((1, 4 * LA_HEADS, c), im3)]

    out = jax.ShapeDtypeStruct((t, LA_V), _F32)
    return pl.pallas_call(
        _delta_body,
        grid=(batch, n),
        in_specs=specs(fwd, fwd3) + specs(bwd, bwd3),
        out_specs=[pl.BlockSpec((c, LA_V), fwd), pl.BlockSpec((c, LA_V), bwd)],
        out_shape=[out, out],
        scratch_shapes=[pltpu.VMEM((2, LA_HEADS, LA_DK, LA_DV), _F32)],
        compiler_params=_params("parallel", "arbitrary"),
        name="delta_rule",
    )(q, k, v, bg, g_t, q, k, v, bg, g_t)


def _attn_body(sink_ref, q_ref, kp_ref, kc_ref, kn_ref, vp_ref, vc_ref, vn_ref, o_ref, *, seq):
    n = pl.program_id(1)
    blk = ATT_BLOCK
    group = ATT_Q_HEADS // ATT_KV_HEADS
    kw = jnp.concatenate([kp_ref[...], kc_ref[...], kn_ref[...]], axis=0)
    vw = jnp.concatenate([vp_ref[...], vc_ref[...], vn_ref[...]], axis=0)
    row = lax.broadcasted_iota(jnp.int32, (blk, 3 * blk), 0)
    col = lax.broadcasted_iota(jnp.int32, (blk, 3 * blk), 1)
    kpos = (n - 1) * blk + col
    valid = (jnp.abs(row + blk - col) <= WINDOW) & (kpos >= 0) & (kpos < seq)
    valid = jnp.concatenate([valid] * group, axis=0)
    for g in range(ATT_KV_HEADS):
        heads = [g * group + j for j in range(group)]
        qs = jnp.concatenate([q_ref[:, h * ATT_HD:(h + 1) * ATT_HD] for h in heads], axis=0)
        s = _mm_nt(qs, kw[:, g * ATT_HD:(g + 1) * ATT_HD]) * (ATT_HD ** -0.5)
        sk = jnp.concatenate([jnp.full((blk, 1), sink_ref[h], _F32) for h in heads], axis=0)
        m = jnp.maximum(jnp.max(jnp.where(valid, s, -1e30), axis=-1, keepdims=True), sk)
        p = jnp.where(valid, jnp.exp(jnp.where(valid, s, 0.0) - m), 0.0)
        denom = jnp.sum(p, axis=-1, keepdims=True) + jnp.exp(sk - m)
        o = _mm(p.astype(_BF16), vw[:, g * ATT_HD:(g + 1) * ATT_HD]) / denom
        for j, h in enumerate(heads):
            o_ref[:, h * ATT_HD:(h + 1) * ATT_HD] = o[j * blk:(j + 1) * blk].astype(o_ref.dtype)


def _window_attention(aqkv, sink, batch, seq):
    t = aqkv.shape[0]
    blk = ATT_BLOCK
    nb = seq // blk
    kcol = ATT_Q // ATT_KV
    prev = lambda b, n: b * nb + jnp.maximum(n - 1, 0)
    cur = lambda b, n: b * nb + n
    nxt = lambda b, n: b * nb + jnp.minimum(n + 1, nb - 1)
    kv_specs = [pl.BlockSpec((blk, ATT_KV), functools.partial(lambda b, n, f, c: (f(b, n), c), f=f, c=c))
                for c in (kcol, kcol + 1) for f in (prev, cur, nxt)]
    return pl.pallas_call(
        functools.partial(_attn_body, seq=seq),
        grid=(batch, nb),
        in_specs=[pl.BlockSpec(memory_space=pltpu.SMEM),
                  pl.BlockSpec((blk, ATT_Q), lambda b, n: (b * nb + n, 0)),
                  *kv_specs],
        out_specs=pl.BlockSpec((blk, ATT_Q), lambda b, n: (b * nb + n, 0)),
        out_shape=jax.ShapeDtypeStruct((t, ATT_Q), _BF16),
        compiler_params=_params("parallel", "parallel"),
        name="window_attention",
    )(sink, *([aqkv] * 7))


def _merge_body(of_ref, ob_ref, z_ref, ga_ref, gb_ref, oatt_ref, x_ref, wa_ref, wb_ref, wo_ref,
                ng_ref, fg_ref, wr_ref, xmid_ref, hn_ref, aff_ref, oa_ref):
    o = of_ref[...] + ob_ref[...]
    z = z_ref[...]
    gate = z * _sigmoid(z)
    for h in range(LA_HEADS):
        hs = slice(h * LA_DV, (h + 1) * LA_DV)
        oh = o[:, hs]
        oh = _rms(oh, ng_ref[...])
        oa_ref[:, hs] = (oh * gate[:, hs]).astype(_BF16)
    merged = ga_ref[...] * _mm(oa_ref[...], wa_ref[...]) + gb_ref[...] * _mm(oatt_ref[...], wb_ref[...])
    xmid = x_ref[...] + _mm(merged.astype(_BF16), wo_ref[...])
    xmid_ref[...] = xmid
    hn = _rms(xmid, fg_ref[...])
    hn_ref[...] = hn.astype(_BF16)
    wh, wm, wl = _split3(wr_ref[...])
    hh, hm, hl = _split3(hn)
    logits = (_mm_nt(wh, hh) + _mm_nt(wh, hm) + _mm_nt(wm, hh)
              + _mm_nt(wh, hl) + _mm_nt(wl, hh) + _mm_nt(wm, hm))
    e = jnp.exp(logits - jnp.max(logits, axis=0, keepdims=True))
    aff_ref[...] = e / jnp.sum(e, axis=0, keepdims=True)


def _merge(o_f, o_b, z, gates, o_att, x, w_a, w_b, w_o, norm_g, ffn_g, w_rt):
    t, d = x.shape
    tm = min(MERGE_TM, t)
    row = lambda i: (i, 0)
    fix = lambda i: (0, 0)
    act = pl.BlockSpec((tm, d), row)
    wspec = pl.BlockSpec((d, d), fix)
    return pl.pallas_call(
        _merge_body,
        grid=(t // tm,),
        in_specs=[act, act, act, act, pl.BlockSpec((tm, d), lambda i: (i, 1)), act, act,
                  wspec, wspec, wspec,
                  pl.BlockSpec((1, LA_DV), fix), pl.BlockSpec((1, d), fix),
                  pl.BlockSpec((N_EXPERTS, d), fix)],
        out_specs=[act, act, pl.BlockSpec((N_EXPERTS, tm), lambda i: (0, i))],
        out_shape=[jax.ShapeDtypeStruct((t, d), _F32), jax.ShapeDtypeStruct((t, d), _BF16),
                   jax.ShapeDtypeStruct((N_EXPERTS, t), _F32)],
        scratch_shapes=[pltpu.VMEM((tm, d), _BF16)],
        compiler_params=_params("parallel"),
        name="merge_router",
    )(o_f, o_b, z, gates, gates, o_att, x, w_a, w_b, w_o, norm_g, ffn_g, w_rt)


def _select_body(aff_ref, p_ref, starts_ref, *, cap, n_blk):
    e = N_EXPERTS
    blk = SELECT_BLK

    def bits_of(x):
        return pltpu.bitcast(x, jnp.int32)

    def bisect(it, prefix):
        cand = prefix | jnp.left_shift(jnp.int32(1), 30 - it)
        cnt = jnp.sum(jnp.where(bits_of(aff_ref[...]) >= cand, 1, 0), axis=1, keepdims=True)
        return jnp.where(cnt >= cap, cand, prefix)

    thr = lax.fori_loop(0, 31, bisect, jnp.zeros((e, 1), jnp.int32))
    n_gt = jnp.sum(jnp.where(bits_of(aff_ref[...]) > thr, 1, 0), axis=1, keepdims=True)
    need = (cap - n_gt).astype(_F32)

    r = lax.broadcasted_iota(jnp.int32, (blk, blk), 0)
    c = lax.broadcasted_iota(jnp.int32, (blk, blk), 1)
    upper = jnp.where(r <= c, 1.0, 0.0).astype(_BF16)
    ones = jnp.ones((SUBLANES, blk), _BF16)

    def block(j, carry):
        eq_c, sel_c, sel_row = carry
        off = pl.multiple_of(j * blk, blk)
        b = bits_of(aff_ref[:, pl.ds(off, blk)])
        gt = b > thr
        eq = b == thr
        eq_b = jnp.where(eq, 1.0, 0.0).astype(_BF16)
        eq_incl = _mm(eq_b, upper) + eq_c
        sel = gt | (eq & (eq_incl <= need))
        sel_b = jnp.where(sel, 1.0, 0.0).astype(_BF16)
        sel_incl = _mm(sel_b, upper) + sel_c
        p_ref[:, pl.ds(off, blk)] = jnp.where(sel, sel_incl, 0.0).astype(jnp.int32)
        starts_ref[pl.ds(j, 1), :] = sel_row[0:1].astype(jnp.int32)
        return (eq_c + jnp.sum(eq_b.astype(_F32), axis=1, keepdims=True),
                sel_c + jnp.sum(sel_b.astype(_F32), axis=1, keepdims=True),
                sel_row + _mm_nt(ones, sel_b))

    zero = jnp.zeros((e, 1), _F32)
    lax.fori_loop(0, n_blk, block, (zero, zero, jnp.zeros((SUBLANES, e), _F32)))


def _select(aff, cap):
    e, t = aff.shape
    n_blk = t // SELECT_BLK
    return pl.pallas_call(
        functools.partial(_select_body, cap=cap, n_blk=n_blk),
        out_shape=[jax.ShapeDtypeStruct((e, t), jnp.int32), jax.ShapeDtypeStruct((n_blk, e), jnp.int32)],
        compiler_params=pltpu.CompilerParams(vmem_limit_bytes=VMEM_LIMIT),
        name="expert_select",
    )(aff)


def _dispatch_body(starts_ref, hn_ref, p_ref, o_ref, x_ref, *, cap, blocks_per_win):
    e = pl.program_id(0)
    i = pl.program_id(1)
    tt, w = DISPATCH_TT, DISPATCH_W

    @pl.when(i == 0)
    def _():
        x_ref[...] = jnp.zeros_like(x_ref)

    slot = lax.broadcasted_iota(jnp.int32, (w, tt), 0)
    for s in range(DISPATCH_SUB):
        win = i * DISPATCH_SUB + s
        start = starts_ref[win * blocks_per_win * N_EXPERTS + e]
        a0 = pl.multiple_of((start // SUBLANES) * SUBLANES, SUBLANES)
        rel = p_ref[pl.ds(e, 1), s * tt:(s + 1) * tt] - 1 - a0
        onehot = jnp.where(rel == slot, 1.0, 0.0).astype(_BF16)
        rows = _mm(onehot, hn_ref[s * tt:(s + 1) * tt, :])
        x_ref[pl.ds(a0, w), :] += rows

    @pl.when(i == pl.num_programs(1) - 1)
    def _():
        o_ref[0] = x_ref[0:cap, :].astype(o_ref.dtype)


def _dispatch(starts, hn, p, cap):
    t, d = hn.shape
    tb = DISPATCH_TT * DISPATCH_SUB
    body = functools.partial(_dispatch_body, cap=cap, blocks_per_win=DISPATCH_TT // SELECT_BLK)
    return pl.pallas_call(
        body,
        grid_spec=pltpu.PrefetchScalarGridSpec(
            num_scalar_prefetch=1,
            grid=(N_EXPERTS, t // tb),
            in_specs=[pl.BlockSpec((tb, d), lambda e, i, s: (i, 0)),
                      pl.BlockSpec((N_EXPERTS, tb), lambda e, i, s: (0, i))],
            out_specs=pl.BlockSpec((1, cap, d), lambda e, i, s: (e, 0, 0)),
            scratch_shapes=[pltpu.VMEM((cap + DISPATCH_W, d), _F32)]),
        out_shape=jax.ShapeDtypeStruct((N_EXPERTS, cap, d), _BF16),
        compiler_params=_params("parallel", "arbitrary"),
        name="expert_dispatch",
    )(starts, hn, p)


def _ffn_body(x_ref, wg_ref, wu_ref, wd_ref, o_ref):
    x = x_ref[0]
    a = _mm(x, wg_ref[0])
    b = _mm(x, wu_ref[0])
    hid = (a * _sigmoid(a) * b).astype(_BF16)
    o_ref[0] = _mm(hid, wd_ref[0]).astype(o_ref.dtype)


def _expert_ffn(xe, w_gate, w_up, w_down):
    e, cap, d = xe.shape
    ff = w_gate.shape[2]
    tm = min(FFN_TM, cap)
    return pl.pallas_call(
        _ffn_body,
        grid=(e, cap // tm),
        in_specs=[pl.BlockSpec((1, tm, d), lambda e, i: (e, i, 0)),
                  pl.BlockSpec((1, d, ff), lambda e, i: (e, 0, 0)),
                  pl.BlockSpec((1, d, ff), lambda e, i: (e, 0, 0)),
                  pl.BlockSpec((1, ff, d), lambda e, i: (e, 0, 0))],
        out_specs=pl.BlockSpec((1, tm, d), lambda e, i: (e, i, 0)),
        out_shape=jax.ShapeDtypeStruct((e, cap, d), _BF16),
        compiler_params=_params("parallel", "arbitrary"),
        name="expert_ffn",
    )(xe, w_gate, w_up, w_down)


def _combine_body(starts_ref, xmid_ref, pt_ref, afft_ref, g_ref, ye_ref, o_ref, buf_ref, sem_ref, *, cap, final_norm):
    i = pl.program_id(0)
    n = pl.num_programs(0)
    tt, w = COMBINE_TT, COMBINE_W

    def window_start(tile, e):
        start = starts_ref[tile * N_EXPERTS + e]
        a0 = jnp.minimum((start // BF16_ROWS) * BF16_ROWS, cap - w)
        return pl.multiple_of(a0, BF16_ROWS)

    def window_copy(tile, e, slot):
        return pltpu.make_async_copy(ye_ref.at[e, pl.ds(window_start(tile, e), w), :],
                                     buf_ref.at[slot, e], sem_ref.at[slot, e])

    @pl.when(i == 0)
    def _():
        for e in range(N_EXPERTS):
            window_copy(0, e, 0).start()

    @pl.when(i + 1 < n)
    def _():
        for e in range(N_EXPERTS):
            window_copy(i + 1, e, lax.rem(i + 1, 2)).start()

    slot = lax.rem(i, 2)
    lane = lax.broadcasted_iota(jnp.int32, (tt, w), 1)
    acc = xmid_ref[...]
    for e in range(N_EXPERTS):
        window_copy(i, e, slot).wait()
        rel = pt_ref[:, e:e + 1] - 1 - window_start(i, e)
        onehot = jnp.where(rel == lane, 1.0, 0.0).astype(_BF16)
        acc = acc + afft_ref[:, e:e + 1] * _mm(onehot, buf_ref[slot, e])
    o_ref[...] = _rms(acc, g_ref[...]) if final_norm else acc


def _combine(starts, xmid, p_t, aff_t, g_final, ye, cap, final_norm):
    t, d = xmid.shape
    tt = COMBINE_TT
    return pl.pallas_call(
        functools.partial(_combine_body, cap=cap, final_norm=final_norm),
        grid_spec=pltpu.PrefetchScalarGridSpec(
            num_scalar_prefetch=1,
            grid=(t // tt,),
            in_specs=[pl.BlockSpec((tt, d), lambda i, s: (i, 0)),
                      pl.BlockSpec((tt, N_EXPERTS), lambda i, s: (i, 0)),
                      pl.BlockSpec((tt, N_EXPERTS), lambda i, s: (i, 0)),
                      pl.BlockSpec((1, d), lambda i, s: (0, 0)),
                      pl.BlockSpec(memory_space=pl.ANY)],
            out_specs=pl.BlockSpec((tt, d), lambda i, s: (i, 0)),
            scratch_shapes=[pltpu.VMEM((2, N_EXPERTS, COMBINE_W, d), _BF16),
                            pltpu.SemaphoreType.DMA((2, N_EXPERTS))]),
        out_shape=jax.ShapeDtypeStruct((t, d), _F32),
        compiler_params=_params("arbitrary"),
        name="expert_combine",
    )(starts, xmid, p_t, aff_t, g_final, ye)


def _rope_tables(seq):
    half = ATT_HD // 2
    inv_freq = ROPE_THETA ** (-jnp.arange(half, dtype=_F32) / half)
    ang = jnp.arange(seq, dtype=_F32)[:, None] * inv_freq[None, :]
    cos, sin = jnp.cos(ang), jnp.sin(ang)
    return jnp.concatenate([cos, cos], axis=-1), jnp.concatenate([-sin, sin], axis=-1)


def _layer(x, batch, seq, w, final_g, final_norm):
    t, d = x.shape
    tm = min(PROJ_TM, t)
    g_mix = w["norm_mix_g"]
    qkv_raw = _project(x, g_mix, w["w_conv"], _F32, _ep_store, 1024)
    z = _project(x, g_mix, w["w_z"], _F32, _ep_store, 1024)
    fix = pl.BlockSpec((1, LANES), lambda i, j: (0, 0))
    bg = _project(x, g_mix, w["w_bg"], _F32, _ep_beta_decay, LANES,
                  extras=(w["a_log"], w["dt_bias"]), extra_specs=(fix, fix))
    rope_spec = pl.BlockSpec((tm, ATT_HD), lambda i, j: (i % (seq // tm), 0))
    aqkv = _project(x, g_mix, w["w_att"], _BF16, _ep_rope, ATT_COLS,
                    extras=w["rope"], extra_specs=(rope_spec, rope_spec))
    gates = _project(x, g_mix, w["w_gates"], _F32, _ep_sigmoid, 1024)

    q, k, v = _delta_prep(qkv_raw, w["conv_w"], seq)
    g_t = bg[:, :4 * LA_HEADS].reshape(t // DELTA_CHUNK, DELTA_CHUNK, 4 * LA_HEADS).transpose(0, 2, 1)
    o_f, o_b = _delta_rule(q, k, v, bg, g_t, batch, seq)
    o_att = _window_attention(aqkv, w["sink"], batch, seq)
    xmid, hn, aff = _merge(o_f, o_b, z, gates, o_att, x, w["w_a"], w["w_b"], w["w_o"],
                           w["la_norm_g"], w["norm_ffn_g"], w["w_rt"])

    cap = max(1, CAPACITY_FACTOR * t // N_EXPERTS)
    assert cap >= COMBINE_W and cap % min(FFN_TM, cap) == 0 and cap % BF16_ROWS == 0
    p, starts = _select(aff, cap)
    starts = starts.reshape(-1)
    xe = _dispatch(starts, hn, p, cap)
    ye = _expert_ffn(xe, w["w_gate"], w["w_up"], w["w_down"])
    return _combine(starts, xmid, p.T, aff.T, final_g, ye, cap, final_norm)


def _layer_weights(l, seq, norm_mix_g, w_in, conv_w, la_a_log, la_dt_bias, la_norm_g, attn_sink, w_branch_a,
                   w_branch_b, w_out, norm_ffn_g, w_router, w_exp_gate, w_exp_up, w_exp_down):
    wi = w_in[l]
    c0 = CONV_CH
    c1 = c0 + LA_V
    c2 = c1 + 4 * LA_HEADS
    c3 = c2 + ATT_COLS
    pad = LANES - 4 * LA_HEADS
    zeros = jnp.zeros((2 * LA_HEADS,), _F32)
    lane_pad = lambda v: jnp.pad(jnp.concatenate([zeros, v.reshape(-1)]), (0, pad)).reshape(1, LANES)
    return {
        "norm_mix_g": norm_mix_g[l].reshape(1, -1),
        "w_conv": wi[:, :c0].astype(_BF16),
        "w_z": wi[:, c0:c1].astype(_BF16),
        "w_bg": jnp.pad(wi[:, c1:c2], ((0, 0), (0, pad))).astype(_BF16),
        "w_att": wi[:, c2:c3].astype(_BF16),
        "w_gates": wi[:, c3:].astype(_BF16),
        "a_log": lane_pad(la_a_log[l]),
        "dt_bias": lane_pad(la_dt_bias[l]),
        "rope": _rope_tables(seq),
        "conv_w": conv_w[l],
        "sink": attn_sink[l],
        "w_a": w_branch_a[l].astype(_BF16),
        "w_b": w_branch_b[l].astype(_BF16),
        "w_o": w_out[l].astype(_BF16),
        "la_norm_g": la_norm_g[l].reshape(1, -1),
        "norm_ffn_g": norm_ffn_g[l].reshape(1, -1),
        "w_rt": w_router[l].T,
        "w_gate": w_exp_gate[l].astype(_BF16),
        "w_up": w_exp_up[l].astype(_BF16),
        "w_down": w_exp_down[l].astype(_BF16),
    }


def kernel(x_prompt, x_sample, norm_mix_g, w_in, conv_w, la_a_log, la_dt_bias, la_norm_g, attn_sink, w_branch_a, w_branch_b, w_out, norm_ffn_g, w_router, w_exp_gate, w_exp_up, w_exp_down, final_norm_g):
    depth = w_in.shape[0]
    layer_args = (norm_mix_g, w_in, conv_w, la_a_log, la_dt_bias, la_norm_g, attn_sink, w_branch_a, w_branch_b,
                  w_out, norm_ffn_g, w_router, w_exp_gate, w_exp_up, w_exp_down)
    final_g = final_norm_g.reshape(1, -1)

    def trunk(x):
        batch, seq, d = x.shape
        h = x.reshape(batch * seq, d)
        for l in range(depth):
            w = _layer_weights(l, seq, *layer_args)
            h = _layer(h, batch, seq, w, final_g, l == depth - 1)
        return h.reshape(batch, seq, d)

    return trunk(x_prompt), trunk(x_sample)
```

```python
import functools

import jax
import jax.numpy as jnp
from jax import lax
from jax.experimental import pallas as pl
from jax.experimental.pallas import tpu as pltpu

D_MODEL = 1024
LA_HEADS = 8
LA_DK = 128
LA_DV = 128
CONV_W = 4
ATT_Q_HEADS = 8
ATT_KV_HEADS = 2
ATT_HD = 128
WINDOW = 128
ROPE_THETA = 10000.0
N_EXPERTS = 16
EXPERT_FF = 2048
CAPACITY_FACTOR = 2
EPS = 1e-6

LA_QK = LA_HEADS * LA_DK
LA_V = LA_HEADS * LA_DV
ATT_Q = ATT_Q_HEADS * ATT_HD
ATT_KV = ATT_KV_HEADS * ATT_HD
CONV_CH = 2 * LA_QK + LA_V
ATT_COLS = ATT_Q + 2 * ATT_KV

LANES = 128
SUBLANES = 8
BF16_ROWS = 16
VMEM_LIMIT = 56 * 1024 * 1024

DELTA_CHUNK = 128
ATT_BLOCK = 128
PROJ_TM = 1024
PREP_TM = 256
MERGE_TM = 256
DISPATCH_TT = 256
DISPATCH_SUB = 4
DISPATCH_EP = 2
DISPATCH_W = DISPATCH_TT + BF16_ROWS
DISPATCH_WS = 64
COMBINE_TT = 128
COMBINE_W = 256
COMBINE_WS = 64
FFN_TM = 512
SELECT_BLK = 128

_BF16 = jnp.bfloat16
_F32 = jnp.float32


def _params(*sem):
    return pltpu.CompilerParams(dimension_semantics=sem, vmem_limit_bytes=VMEM_LIMIT)


def _rms(x, g):
    return x * lax.rsqrt(jnp.mean(x * x, axis=-1, keepdims=True) + EPS) * g


def _sigmoid(x):
    return 1.0 / (1.0 + jnp.exp(-x))


def _mm(a, b):
    return jnp.dot(a, b, preferred_element_type=_F32)


def _mm_nt(a, b):
    return lax.dot_general(a, b, (((1,), (1,)), ((), ())), preferred_element_type=_F32)


def _mm_tn(a, b):
    return lax.dot_general(a, b, (((0,), (0,)), ((), ())), preferred_element_type=_F32)


def _split3(x):
    h = x.astype(_BF16)
    r = x - h.astype(_F32)
    m = r.astype(_BF16)
    l = (r - m.astype(_F32)).astype(_BF16)
    return h, m, l


def _mm_exact_lhs(a01, x):
    h, m, l = _split3(x)
    return _mm(a01, h) + _mm(a01, m) + _mm(a01, l)


def _proj_body(x_ref, g_ref, w_ref, *refs, epilogue, n_extra):
    extra, o_ref, hn_ref = refs[:n_extra], refs[n_extra], refs[n_extra + 1]

    @pl.when(pl.program_id(1) == 0)
    def _():
        hn_ref[...] = _rms(x_ref[...], g_ref[...]).astype(_BF16)

    epilogue(_mm(hn_ref[...], w_ref[...]), o_ref, *extra)


def _ep_store(acc, o_ref):
    o_ref[...] = acc.astype(o_ref.dtype)


def _ep_sigmoid(acc, o_ref):
    o_ref[...] = _sigmoid(acc).astype(o_ref.dtype)


def _ep_beta_decay(acc, o_ref, a_ref, dt_ref):
    z = acc + dt_ref[...]
    sp = jnp.maximum(z, 0.0) + jnp.log1p(jnp.exp(-jnp.abs(z)))
    lane = lax.broadcasted_iota(jnp.int32, acc.shape, 1)
    g = jnp.where(lane < 4 * LA_HEADS, -jnp.exp(a_ref[...]) * sp, 0.0)
    o_ref[...] = jnp.where(lane < 2 * LA_HEADS, _sigmoid(acc), g)


def _ep_rope(acc, o_ref, cos_ref, sin_ref):
    c, s = cos_ref[...], sin_ref[...]
    for h in range(ATT_Q_HEADS + ATT_KV_HEADS):
        sl = slice(h * ATT_HD, (h + 1) * ATT_HD)
        x = acc[:, sl]
        o_ref[:, sl] = (x * c + pltpu.roll(x, ATT_HD // 2, 1) * s).astype(o_ref.dtype)
    o_ref[:, ATT_Q + ATT_KV:] = acc[:, ATT_Q + ATT_KV:].astype(o_ref.dtype)


def _project(x, g, w, out_dtype, epilogue, tn, extras=(), extra_specs=()):
    t, d = x.shape
    n = w.shape[1]
    tm = min(PROJ_TM, t)
    body = functools.partial(_proj_body, epilogue=epilogue, n_extra=len(extras))
    return pl.pallas_call(
        body,
        grid=(t // tm, n // tn),
        in_specs=[pl.BlockSpec((tm, d), lambda i, j: (i, 0)),
                  pl.BlockSpec((1, d), lambda i, j: (0, 0)),
                  pl.BlockSpec((d, tn), lambda i, j: (0, j)),
                  *extra_specs],
        out_specs=pl.BlockSpec((tm, tn), lambda i, j: (i, j)),
        out_shape=jax.ShapeDtypeStruct((t, n), out_dtype),
        scratch_shapes=[pltpu.VMEM((tm, d), _BF16)],
        compiler_params=_params("parallel", "arbitrary"),
        name="in_proj",
    )(x, g, w, *extras)


def _prep_body(cur_ref, prev_ref, next_ref, w_ref, q_ref, k_ref, v_ref, ext_ref, *, tiles_per_seq, tm):
    i = pl.program_id(0)
    pos = lax.rem(i, tiles_per_seq)
    keep_prev = jnp.where(pos == 0, 0.0, 1.0)
    keep_next = jnp.where(pos == tiles_per_seq - 1, 0.0, 1.0)
    hr = BF16_ROWS
    for c, o_ref in enumerate((q_ref, k_ref, v_ref)):
        sl = slice(c * LA_QK, (c + 1) * LA_QK)
        ext_ref[0:hr, :] = prev_ref[:, sl].astype(_F32) * keep_prev
        ext_ref[hr:hr + tm, :] = cur_ref[:, sl].astype(_F32)
        ext_ref[hr + tm:, :] = next_ref[:, sl].astype(_F32) * keep_next
        w = w_ref[:, sl]
        y = (ext_ref[hr - 2:hr - 2 + tm, :] * w[0:1]
             + ext_ref[hr - 1:hr - 1 + tm, :] * w[1:2]
             + ext_ref[hr:hr + tm, :] * w[2:3]
             + ext_ref[hr + 1:hr + 1 + tm, :] * w[3:4])
        y = y * _sigmoid(y)
        if c == 2:
            o_ref[...] = y.astype(o_ref.dtype)
            continue
        scale = LA_DK ** -0.5 if c == 0 else 1.0
        for h in range(LA_HEADS):
            hs = slice(h * LA_DK, (h + 1) * LA_DK)
            yh = y[:, hs]
            yh = yh * lax.rsqrt(jnp.sum(yh * yh, axis=-1, keepdims=True) + EPS)
            o_ref[:, hs] = (yh * scale).astype(o_ref.dtype)


def _delta_prep(qkv_raw, conv_w, seq):
    t = qkv_raw.shape[0]
    tm = min(PREP_TM, seq)
    halo_blocks = tm // BF16_ROWS
    n_halo = t // BF16_ROWS
    body = functools.partial(_prep_body, tiles_per_seq=seq // tm, tm=tm)
    out = jax.ShapeDtypeStruct((t, LA_QK), _BF16)
    return pl.pallas_call(
        body,
        grid=(t // tm,),
        in_specs=[pl.BlockSpec((tm, CONV_CH), lambda i: (i, 0)),
                  pl.BlockSpec((BF16_ROWS, CONV_CH), lambda i: (jnp.maximum(i * halo_blocks - 1, 0), 0)),
                  pl.BlockSpec((BF16_ROWS, CONV_CH), lambda i: (jnp.minimum((i + 1) * halo_blocks, n_halo - 1), 0)),
                  pl.BlockSpec((CONV_W, CONV_CH), lambda i: (0, 0))],
        out_specs=[pl.BlockSpec((tm, LA_QK), lambda i: (i, 0))] * 3,
        out_shape=[out, out, out],
        scratch_shapes=[pltpu.VMEM((tm + 2 * BF16_ROWS, LA_QK), _F32)],
        compiler_params=_params("parallel"),
        name="delta_prep",
    )(qkv_raw, qkv_raw, qkv_raw, conv_w)


def _delta_body(qf, kf, vf, bgf, gtf, qb, kb, vb, bgb, gtb, of_ref, ob_ref, state_ref):
    @pl.when(pl.program_id(1) == 0)
    def _():
        state_ref[...] = jnp.zeros_like(state_ref)

    c = DELTA_CHUNK
    row = lax.broadcasted_iota(jnp.int32, (c, c), 0)
    col = lax.broadcasted_iota(jnp.int32, (c, c), 1)
    eye = jnp.where(row == col, 1.0, 0.0)
    refs = ((qf, kf, vf, bgf, gtf, of_ref), (qb, kb, vb, bgb, gtb, ob_ref))

    chains = []
    for d, (q_ref, k_ref, v_ref, bg_ref, gt_ref, o_ref) in enumerate(refs):
        incl = (row >= col) if d == 0 else (row <= col)
        strict = (row > col) if d == 0 else (row < col)
        incl_b = jnp.where(incl, 1.0, 0.0).astype(_BF16)
        incl_t = jnp.where((row <= col) if d == 0 else (row >= col), 1.0, 0.0).astype(_BF16)
        bg = bg_ref[...]
        gc_cols = _mm_exact_lhs(incl_b, bg)
        h3 = _split3(gt_ref[0])
        gc_rows = _mm(h3[0], incl_t) + _mm(h3[1], incl_t) + _mm(h3[2], incl_t)
        total_row = c - 1 if d == 0 else 0
        for h in range(LA_HEADS):
            bcol = LA_HEADS * d + h
            gcol = 2 * LA_HEADS + LA_HEADS * d + h
            chains.append(dict(
                d=d, h=h, hs=slice(h * LA_DK, (h + 1) * LA_DK), incl=incl, strict=strict,
                q_ref=q_ref, k_ref=k_ref, v_ref=v_ref, o_ref=o_ref,
                beta=bg[:, bcol:bcol + 1], gc=gc_cols[:, gcol:gcol + 1], gc_r=gc_rows[gcol:gcol + 1, :],
                g_tot=gc_cols[total_row:total_row + 1, gcol:gcol + 1]))

    for ch in chains:
        q = ch["q_ref"][:, ch["hs"]].astype(_F32)
        k = ch["k_ref"][:, ch["hs"]]
        kf32 = k.astype(_F32)
        ch["kb"] = kf32 * ch["beta"]
        ch["egc"] = jnp.exp(ch["gc"])
        ch["a_qk"] = _mm_nt(jnp.concatenate([ch["kb"], q], axis=0).astype(_BF16), k)
        ch["qg"] = (q * ch["egc"]).astype(_BF16)
        ch["k_dec"] = (kf32 * jnp.exp(ch["g_tot"] - ch["gc"])).astype(_BF16)

    for ch in chains:
        incl = ch["incl"]
        decay = jnp.where(incl, jnp.exp(jnp.where(incl, ch["gc"] - ch["gc_r"], 0.0)), 0.0)
        l_mat = jnp.where(ch["strict"], ch["a_qk"][:c] * decay, 0.0)
        ch["qk"] = (ch["a_qk"][c:] * decay).astype(_BF16)
        ch["y"] = eye - l_mat
        l_b = l_mat.astype(_BF16)
        ch["m_b"] = _mm(l_b, l_b).astype(_BF16)
    p = 2
    while p < c:
        last = 2 * p >= c
        for ch in chains:
            lhs = ch["y"].astype(_BF16) if last else jnp.concatenate([ch["m_b"], ch["y"].astype(_BF16)], axis=0)
            prod = _mm(lhs, ch["m_b"])
            if last:
                ch["y"] = ch["y"] + prod
            else:
                ch["m_b"] = prod[:c].astype(_BF16)
                ch["y"] = ch["y"] + prod[c:]
        p *= 2

    for ch in chains:
        v = ch["v_ref"][:, ch["hs"]].astype(_F32)
        rhs = jnp.concatenate([v * ch["beta"], ch["kb"] * ch["egc"]], axis=1).astype(_BF16)
        ch["uw"] = _mm(ch["y"].astype(_BF16), rhs)
    for ch in chains:
        ch["state"] = state_ref[ch["d"], ch["h"]]
        w_b = ch["uw"][:, LA_DV:].astype(_BF16)
        ch["ws_qs"] = _mm(jnp.concatenate([w_b, ch["qg"]], axis=0), ch["state"].astype(_BF16))
    for ch in chains:
        v_new_b = (ch["uw"][:, :LA_DV] - ch["ws_qs"][:c]).astype(_BF16)
        ch["o_ref"][:, ch["hs"]] = ch["ws_qs"][c:] + _mm(ch["qk"], v_new_b)
        state_ref[ch["d"], ch["h"]] = ch["state"] * jnp.exp(ch["g_tot"]) + _mm_tn(ch["k_dec"], v_new_b)


def _delta_rule(q, k, v, bg, g_t, batch, seq):
    t = q.shape[0]
    c = DELTA_CHUNK
    n = seq // c
    fwd = lambda b, j: (b * n + j, 0)
    bwd = lambda b, j: (b * n + n - 1 - j, 0)
    fwd3 = lambda b, j: (b * n + j, 0, 0)
    bwd3 = lambda b, j: (b * n + n - 1 - j, 0, 0)

    def specs(im, im3):
        return [pl.BlockSpec((c, LA_QK), im), pl.BlockSpec((c, LA_QK), im), pl.BlockSpec((c, LA_V), im),
                pl.BlockSpec((c, LANES), im), pl.BlockSpec((1, 4 * LA_HEADS, c), im3)]

    out = jax.ShapeDtypeStruct((t, LA_V), _F32)
    return pl.pallas_call(
        _delta_body,
        grid=(batch, n),
        in_specs=specs(fwd, fwd3) + specs(bwd, bwd3),
        out_specs=[pl.BlockSpec((c, LA_V), fwd), pl.BlockSpec((c, LA_V), bwd)],
        out_shape=[out, out],
        scratch_shapes=[pltpu.VMEM((2, LA_HEADS, LA_DK, LA_DV), _F32)],
        compiler_params=_params("parallel", "arbitrary"),
        name="delta_rule",
    )(q, k, v, bg, g_t, q, k, v, bg, g_t)


def _attn_body(sink_ref, q_ref, kp_ref, kc_ref, kn_ref, vp_ref, vc_ref, vn_ref, o_ref, *, seq):
    n = pl.program_id(1)
    blk = ATT_BLOCK
    group = ATT_Q_HEADS // ATT_KV_HEADS
    kw = jnp.concatenate([kp_ref[...], kc_ref[...], kn_ref[...]], axis=0)
    vw = jnp.concatenate([vp_ref[...], vc_ref[...], vn_ref[...]], axis=0)
    row = lax.broadcasted_iota(jnp.int32, (blk, 3 * blk), 0)
    col = lax.broadcasted_iota(jnp.int32, (blk, 3 * blk), 1)
    kpos = (n - 1) * blk + col
    valid = (jnp.abs(row + blk - col) <= WINDOW) & (kpos >= 0) & (kpos < seq)
    valid = jnp.concatenate([valid] * group, axis=0)
    for g in range(ATT_KV_HEADS):
        heads = [g * group + j for j in range(group)]
        qs = jnp.concatenate([q_ref[:, h * ATT_HD:(h + 1) * ATT_HD] for h in heads], axis=0)
        s = _mm_nt(qs, kw[:, g * ATT_HD:(g + 1) * ATT_HD]) * (ATT_HD ** -0.5)
        sk = jnp.concatenate([jnp.full((blk, 1), sink_ref[h], _F32) for h in heads], axis=0)
        m = jnp.maximum(jnp.max(jnp.where(valid, s, -1e30), axis=-1, keepdims=True), sk)
        p = jnp.where(valid, jnp.exp(jnp.where(valid, s, 0.0) - m), 0.0)
        denom = jnp.sum(p, axis=-1, keepdims=True) + jnp.exp(sk - m)
        o = _mm(p.astype(_BF16), vw[:, g * ATT_HD:(g + 1) * ATT_HD]) / denom
        for j, h in enumerate(heads):
            o_ref[:, h * ATT_HD:(h + 1) * ATT_HD] = o[j * blk:(j + 1) * blk].astype(o_ref.dtype)


def _window_attention(aqkv, sink, batch, seq):
    t = aqkv.shape[0]
    blk = ATT_BLOCK
    nb = seq // blk
    kcol = ATT_Q // ATT_KV
    prev = lambda b, n: b * nb + jnp.maximum(n - 1, 0)
    cur = lambda b, n: b * nb + n
    nxt = lambda b, n: b * nb + jnp.minimum(n + 1, nb - 1)
    kv_specs = [pl.BlockSpec((blk, ATT_KV), functools.partial(lambda b, n, f, c: (f(b, n), c), f=f, c=c))
                for c in (kcol, kcol + 1) for f in (prev, cur, nxt)]
    return pl.pallas_call(
        functools.partial(_attn_body, seq=seq),
        grid=(batch, nb),
        in_specs=[pl.BlockSpec(memory_space=pltpu.SMEM),
                  pl.BlockSpec((blk, ATT_Q), lambda b, n: (b * nb + n, 0)),
                  *kv_specs],
        out_specs=pl.BlockSpec((blk, ATT_Q), lambda b, n: (b * nb + n, 0)),
        out_shape=jax.ShapeDtypeStruct((t, ATT_Q), _BF16),
        compiler_params=_params("parallel", "parallel"),
        name="window_attention",
    )(sink, *([aqkv] * 7))


def _merge_body(of_ref, ob_ref, z_ref, ga_ref, gb_ref, oatt_ref, x_ref, wa_ref, wb_ref, wo_ref,
                ng_ref, fg_ref, wr_ref, xmid_ref, hn_ref, aff_ref, oa_ref):
    o = of_ref[...] + ob_ref[...]
    z = z_ref[...].astype(_F32)
    gate = z * _sigmoid(z)
    for h in range(LA_HEADS):
        hs = slice(h * LA_DV, (h + 1) * LA_DV)
        oh = o[:, hs]
        oh = _rms(oh, ng_ref[...])
        oa_ref[:, hs] = (oh * gate[:, hs]).astype(_BF16)
    merged = (ga_ref[...].astype(_F32) * _mm(oa_ref[...], wa_ref[...])
              + gb_ref[...].astype(_F32) * _mm(oatt_ref[...], wb_ref[...]))
    xmid = x_ref[...] + _mm(merged.astype(_BF16), wo_ref[...])
    xmid_ref[...] = xmid
    hn = _rms(xmid, fg_ref[...])
    hn_ref[...] = hn.astype(_BF16)
    wh, wm, wl = _split3(wr_ref[...])
    hh, hm, hl = _split3(hn)
    logits = (_mm_nt(wh, hh) + _mm_nt(wh, hm) + _mm_nt(wm, hh)
              + _mm_nt(wh, hl) + _mm_nt(wl, hh) + _mm_nt(wm, hm))
    e = jnp.exp(logits - jnp.max(logits, axis=0, keepdims=True))
    aff_ref[...] = e / jnp.sum(e, axis=0, keepdims=True)


def _merge(o_f, o_b, z, gates, o_att, x, w_a, w_b, w_o, norm_g, ffn_g, w_rt):
    t, d = x.shape
    tm = min(MERGE_TM, t)
    row = lambda i: (i, 0)
    fix = lambda i: (0, 0)
    act = pl.BlockSpec((tm, d), row)
    wspec = pl.BlockSpec((d, d), fix)
    return pl.pallas_call(
        _merge_body,
        grid=(t // tm,),
        in_specs=[act, act, act, act, pl.BlockSpec((tm, d), lambda i: (i, 1)), act, act,
                  wspec, wspec, wspec,
                  pl.BlockSpec((1, LA_DV), fix), pl.BlockSpec((1, d), fix),
                  pl.BlockSpec((N_EXPERTS, d), fix)],
        out_specs=[act, act, pl.BlockSpec((N_EXPERTS, tm), lambda i: (0, i))],
        out_shape=[jax.ShapeDtypeStruct((t, d), _F32), jax.ShapeDtypeStruct((t, d), _BF16),
                   jax.ShapeDtypeStruct((N_EXPERTS, t), _F32)],
        scratch_shapes=[pltpu.VMEM((tm, d), _BF16)],
        compiler_params=_params("parallel"),
        name="merge_router",
    )(o_f, o_b, z, gates, gates, o_att, x, w_a, w_b, w_o, norm_g, ffn_g, w_rt)


def _select_body(aff_ref, p_ref, starts_ref, *, cap, n_blk):
    e = N_EXPERTS
    blk = SELECT_BLK

    def bits_of(x):
        return pltpu.bitcast(x, jnp.int32)

    def bisect(it, prefix):
        cand = prefix | jnp.left_shift(jnp.int32(1), 30 - it)
        cnt = jnp.sum(jnp.where(bits_of(aff_ref[...]) >= cand, 1, 0), axis=1, keepdims=True)
        return jnp.where(cnt >= cap, cand, prefix)

    thr = lax.fori_loop(0, 31, bisect, jnp.zeros((e, 1), jnp.int32))
    n_gt = jnp.sum(jnp.where(bits_of(aff_ref[...]) > thr, 1, 0), axis=1, keepdims=True)
    need = (cap - n_gt).astype(_F32)

    r = lax.broadcasted_iota(jnp.int32, (blk, blk), 0)
    c = lax.broadcasted_iota(jnp.int32, (blk, blk), 1)
    upper = jnp.where(r <= c, 1.0, 0.0).astype(_BF16)
    ones = jnp.ones((SUBLANES, blk), _BF16)

    def block(j, carry):
        eq_c, sel_c, sel_row = carry
        off = pl.multiple_of(j * blk, blk)
        b = bits_of(aff_ref[:, pl.ds(off, blk)])
        gt = b > thr
        eq = b == thr
        eq_b = jnp.where(eq, 1.0, 0.0).astype(_BF16)
        eq_incl = _mm(eq_b, upper) + eq_c
        sel = gt | (eq & (eq_incl <= need))
        sel_b = jnp.where(sel, 1.0, 0.0).astype(_BF16)
        sel_incl = _mm(sel_b, upper) + sel_c
        p_ref[:, pl.ds(off, blk)] = jnp.where(sel, sel_incl, 0.0).astype(jnp.int32)
        starts_ref[pl.ds(j, 1), :] = sel_row[0:1].astype(jnp.int32)
        return (eq_c + jnp.sum(eq_b.astype(_F32), axis=1, keepdims=True),
                sel_c + jnp.sum(sel_b.astype(_F32), axis=1, keepdims=True),
                sel_row + _mm_nt(ones, sel_b))

    zero = jnp.zeros((e, 1), _F32)
    _, _, totals = lax.fori_loop(0, n_blk, block, (zero, zero, jnp.zeros((SUBLANES, e), _F32)))
    starts_ref[n_blk:n_blk + 1, :] = totals[0:1].astype(jnp.int32)


def _select(aff, cap):
    e, t = aff.shape
    n_blk = t // SELECT_BLK
    return pl.pallas_call(
        functools.partial(_select_body, cap=cap, n_blk=n_blk),
        out_shape=[jax.ShapeDtypeStruct((e, t), jnp.int32), jax.ShapeDtypeStruct((n_blk + 1, e), jnp.int32)],
        compiler_params=pltpu.CompilerParams(vmem_limit_bytes=VMEM_LIMIT),
        name="expert_select",
    )(aff)


def _dispatch_body(starts_ref, hn_ref, p_ref, o_ref, x_ref, *, cap, blocks_per_win):
    eb = pl.program_id(0)
    i = pl.program_id(1)
    tt = DISPATCH_TT
    stride = blocks_per_win * N_EXPERTS

    @pl.when(i == 0)
    def _():
        x_ref[...] = jnp.zeros_like(x_ref)

    for s in range(DISPATCH_SUB):
        win = i * DISPATCH_SUB + s
        hn_s = hn_ref[s * tt:(s + 1) * tt, :]
        for j in range(DISPATCH_EP):
            e = eb * DISPATCH_EP + j
            start = starts_ref[win * stride + e]
            end = starts_ref[(win + 1) * stride + e]
            a0 = pl.multiple_of((start // SUBLANES) * SUBLANES, SUBLANES)
            rel = p_ref[pl.ds(e, 1), s * tt:(s + 1) * tt] - 1 - a0
            small = end - a0 <= DISPATCH_WS

            def gather(w, j=j, a0=a0, rel=rel, hn_s=hn_s):
                slot = lax.broadcasted_iota(jnp.int32, (w, tt), 0)
                onehot = jnp.where(rel == slot, 1.0, 0.0).astype(_BF16)
                x_ref[j, pl.ds(a0, w), :] += _mm(onehot, hn_s)

            pl.when(small)(functools.partial(gather, DISPATCH_WS))
            pl.when(jnp.logical_not(small))(functools.partial(gather, DISPATCH_W))

    @pl.when(i == pl.num_programs(1) - 1)
    def _():
        o_ref[...] = x_ref[:, 0:cap, :].astype(o_ref.dtype)


def _dispatch(starts, hn, p, cap):
    t, d = hn.shape
    tb = DISPATCH_TT * DISPATCH_SUB
    ep = DISPATCH_EP
    body = functools.partial(_dispatch_body, cap=cap, blocks_per_win=DISPATCH_TT // SELECT_BLK)
    return pl.pallas_call(
        body,
        grid_spec=pltpu.PrefetchScalarGridSpec(
            num_scalar_prefetch=1,
            grid=(N_EXPERTS // ep, t // tb),
            in_specs=[pl.BlockSpec((tb, d), lambda e, i, s: (i, 0)),
                      pl.BlockSpec((N_EXPERTS, tb), lambda e, i, s: (0, i))],
            out_specs=pl.BlockSpec((ep, cap, d), lambda e, i, s: (e, 0, 0)),
            scratch_shapes=[pltpu.VMEM((ep, cap + DISPATCH_W, d), _F32)]),
        out_shape=jax.ShapeDtypeStruct((N_EXPERTS, cap, d), _BF16),
        compiler_params=_params("parallel", "arbitrary"),
        name="expert_dispatch",
    )(starts, hn, p)


def _ffn_body(x_ref, wg_ref, wu_ref, wd_ref, o_ref):
    x = x_ref[0]
    a = _mm(x, wg_ref[0])
    b = _mm(x, wu_ref[0])
    hid = (a * _sigmoid(a) * b).astype(_BF16)
    o_ref[0] = _mm(hid, wd_ref[0]).astype(o_ref.dtype)


def _expert_ffn(xe, w_gate, w_up, w_down):
    e, cap, d = xe.shape
    ff = w_gate.shape[2]
    tm = min(FFN_TM, cap)
    return pl.pallas_call(
        _ffn_body,
        grid=(e, cap // tm),
        in_specs=[pl.BlockSpec((1, tm, d), lambda e, i: (e, i, 0)),
                  pl.BlockSpec((1, d, ff), lambda e, i: (e, 0, 0)),
                  pl.BlockSpec((1, d, ff), lambda e, i: (e, 0, 0)),
                  pl.BlockSpec((1, ff, d), lambda e, i: (e, 0, 0))],
        out_specs=pl.BlockSpec((1, tm, d), lambda e, i: (e, i, 0)),
        out_shape=jax.ShapeDtypeStruct((e, cap, d), _BF16),
        compiler_params=_params("parallel", "arbitrary"),
        name="expert_ffn",
    )(xe, w_gate, w_up, w_down)


def _combine_body(starts_ref, xmid_ref, pt_ref, afft_ref, g_ref, ye_ref, o_ref, sbuf_ref, bbuf_ref, sem_ref, acc_ref,
                  *, cap, final_norm):
    i = pl.program_id(0)
    n = pl.num_programs(0)
    tt = COMBINE_TT

    def window(tile, e, slot):
        start = starts_ref[tile * N_EXPERTS + e]
        end = starts_ref[(tile + 1) * N_EXPERTS + e]
        aligned = (start // BF16_ROWS) * BF16_ROWS
        out = []
        for w, buf in ((COMBINE_WS, sbuf_ref), (COMBINE_W, bbuf_ref)):
            a0 = pl.multiple_of(jnp.minimum(aligned, cap - w), BF16_ROWS)
            dst = buf.at[slot, e]
            out.append((a0, pltpu.make_async_copy(ye_ref.at[e, pl.ds(a0, w), :], dst, sem_ref.at[slot, e]), dst))
        return end - out[0][0] <= COMBINE_WS, out[0], out[1]

    def start_tile(tile, slot):
        for e in range(N_EXPERTS):
            use_short, short, long = window(tile, e, slot)
            pl.when(use_short)(short[1].start)
            pl.when(jnp.logical_not(use_short))(long[1].start)

    @pl.when(i == 0)
    def _():
        start_tile(0, 0)

    @pl.when(i + 1 < n)
    def _():
        start_tile(i + 1, lax.rem(i + 1, 2))

    slot = lax.rem(i, 2)
    acc_ref[...] = xmid_ref[...]
    for e in range(N_EXPERTS):
        use_short, short, long = window(i, e, slot)

        def expand(win, w, e=e):
            a0, copy, buf = win
            copy.wait()
            lane = lax.broadcasted_iota(jnp.int32, (tt, w), 1)
            rel = pt_ref[:, e:e + 1] - 1 - a0
            onehot = jnp.where(rel == lane, 1.0, 0.0).astype(_BF16)
            acc_ref[...] += afft_ref[:, e:e + 1] * _mm(onehot, buf[...])

        pl.when(use_short)(functools.partial(expand, short, COMBINE_WS))
        pl.when(jnp.logical_not(use_short))(functools.partial(expand, long, COMBINE_W))
    acc = acc_ref[...]
    o_ref[...] = _rms(acc, g_ref[...]) if final_norm else acc


def _combine(starts, xmid, p_t, aff_t, g_final, ye, cap, final_norm):
    t, d = xmid.shape
    tt = COMBINE_TT
    return pl.pallas_call(
        functools.partial(_combine_body, cap=cap, final_norm=final_norm),
        grid_spec=pltpu.PrefetchScalarGridSpec(
            num_scalar_prefetch=1,
            grid=(t // tt,),
            in_specs=[pl.BlockSpec((tt, d), lambda i, s: (i, 0)),
                      pl.BlockSpec((tt, N_EXPERTS), lambda i, s: (i, 0)),
                      pl.BlockSpec((tt, N_EXPERTS), lambda i, s: (i, 0)),
                      pl.BlockSpec((1, d), lambda i, s: (0, 0)),
                      pl.BlockSpec(memory_space=pl.ANY)],
            out_specs=pl.BlockSpec((tt, d), lambda i, s: (i, 0)),
            scratch_shapes=[pltpu.VMEM((2, N_EXPERTS, COMBINE_WS, d), _BF16),
                            pltpu.VMEM((2, N_EXPERTS, COMBINE_W, d), _BF16),
                            pltpu.SemaphoreType.DMA((2, N_EXPERTS)),
                            pltpu.VMEM((tt, d), _F32)]),
        out_shape=jax.ShapeDtypeStruct((t, d), _F32),
        compiler_params=_params("arbitrary"),
        name="expert_combine",
    )(starts, xmid, p_t, aff_t, g_final, ye)


def _rope_tables(seq):
    half = ATT_HD // 2
    inv_freq = ROPE_THETA ** (-jnp.arange(half, dtype=_F32) / half)
    ang = jnp.arange(seq, dtype=_F32)[:, None] * inv_freq[None, :]
    cos, sin = jnp.cos(ang), jnp.sin(ang)
    return jnp.concatenate([cos, cos], axis=-1), jnp.concatenate([-sin, sin], axis=-1)


def _layer(x, batch, seq, w, final_g, final_norm):
    t, d = x.shape
    tm = min(PROJ_TM, t)
    g_mix = w["norm_mix_g"]
    qkv_raw = _project(x, g_mix, w["w_conv"], _BF16, _ep_store, 1024)
    z = _project(x, g_mix, w["w_z"], _BF16, _ep_store, 1024)
    fix = pl.BlockSpec((1, LANES), lambda i, j: (0, 0))
    bg = _project(x, g_mix, w["w_bg"], _F32, _ep_beta_decay, LANES,
                  extras=(w["a_log"], w["dt_bias"]), extra_specs=(fix, fix))
    rope_spec = pl.BlockSpec((tm, ATT_HD), lambda i, j: (i % (seq // tm), 0))
    aqkv = _project(x, g_mix, w["w_att"], _BF16, _ep_rope, ATT_COLS,
                    extras=w["rope"], extra_specs=(rope_spec, rope_spec))
    gates = _project(x, g_mix, w["w_gates"], _BF16, _ep_sigmoid, 1024)

    q, k, v = _delta_prep(qkv_raw, w["conv_w"], seq)
    g_t = bg[:, :4 * LA_HEADS].reshape(t // DELTA_CHUNK, DELTA_CHUNK, 4 * LA_HEADS).transpose(0, 2, 1)
    o_f, o_b = _delta_rule(q, k, v, bg, g_t, batch, seq)
    o_att = _window_attention(aqkv, w["sink"], batch, seq)
    xmid, hn, aff = _merge(o_f, o_b, z, gates, o_att, x, w["w_a"], w["w_b"], w["w_o"],
                           w["la_norm_g"], w["norm_ffn_g"], w["w_rt"])

    cap = max(1, CAPACITY_FACTOR * t // N_EXPERTS)
    assert cap >= COMBINE_W and cap % min(FFN_TM, cap) == 0 and cap % BF16_ROWS == 0
    p, starts = _select(aff, cap)
    starts = starts.reshape(-1)
    xe = _dispatch(starts, hn, p, cap)
    ye = _expert_ffn(xe, w["w_gate"], w["w_up"], w["w_down"])
    return _combine(starts, xmid, p.T, aff.T, final_g, ye, cap, final_norm)


def _layer_weights(l, seq, norm_mix_g, w_in, conv_w, la_a_log, la_dt_bias, la_norm_g, attn_sink, w_branch_a,
                   w_branch_b, w_out, norm_ffn_g, w_router, w_exp_gate, w_exp_up, w_exp_down):
    wi = w_in[l]
    c0 = CONV_CH
    c1 = c0 + LA_V
    c2 = c1 + 4 * LA_HEADS
    c3 = c2 + ATT_COLS
    pad = LANES - 4 * LA_HEADS
    zeros = jnp.zeros((2 * LA_HEADS,), _F32)
    lane_pad = lambda v: jnp.pad(jnp.concatenate([zeros, v.reshape(-1)]), (0, pad)).reshape(1, LANES)
    return {
        "norm_mix_g": norm_mix_g[l].reshape(1, -1),
        "w_conv": wi[:, :c0].astype(_BF16),
        "w_z": wi[:, c0:c1].astype(_BF16),
        "w_bg": jnp.pad(wi[:, c1:c2], ((0, 0), (0, pad))).astype(_BF16),
        "w_att": wi[:, c2:c3].astype(_BF16),
        "w_gates": wi[:, c3:].astype(_BF16),
        "a_log": lane_pad(la_a_log[l]),
        "dt_bias": lane_pad(la_dt_bias[l]),
        "rope": _rope_tables(seq),
        "conv_w": conv_w[l],
        "sink": attn_sink[l],
        "w_a": w_branch_a[l].astype(_BF16),
        "w_b": w_branch_b[l].astype(_BF16),
        "w_o": w_out[l].astype(_BF16),
        "la_norm_g": la_norm_g[l].reshape(1, -1),
        "norm_ffn_g": norm_ffn_g[l].reshape(1, -1),
        "w_rt": w_router[l].T,
        "w_gate": w_exp_gate[l].astype(_BF16),
        "w_up": w_exp_up[l].astype(_BF16),
        "w_down": w_exp_down[l].astype(_BF16),
    }


def kernel(x_prompt, x_sample, norm_mix_g, w_in, conv_w, la_a_log, la_dt_bias, la_norm_g, attn_sink, w_branch_a, w_branch_b, w_out, norm_ffn_g, w_router, w_exp_gate, w_exp_up, w_exp_down, final_norm_g):
    depth = w_in.shape[0]
    layer_args = (norm_mix_g, w_in, conv_w, la_a_log, la_dt_bias, la_norm_g, attn_sink, w_branch_a, w_branch_b,
                  w_out, norm_ffn_g, w_router, w_exp_gate, w_exp_up, w_exp_down)
    final_g = final_norm_g.reshape(1, -1)

    def trunk(x):
        batch, seq, d = x.shape
        h = x.reshape(batch * seq, d)
        for l in range(depth):
            w = _layer_weights(l, seq, *layer_args)
            h = _layer(h, batch, seq, w, final_g, l == depth - 1)
        return h.reshape(batch, seq, d)

    return trunk(x_prompt), trunk(x_sample)
```

```python
import functools

import jax
import jax.numpy as jnp
from jax import lax
from jax.experimental import pallas as pl
from jax.experimental.pallas import tpu as pltpu

D_MODEL = 1024
LA_HEADS = 8
LA_DK = 128
LA_DV = 128
CONV_W = 4
ATT_Q_HEADS = 8
ATT_KV_HEADS = 2
ATT_HD = 128
WINDOW = 128
ROPE_THETA = 10000.0
N_EXPERTS = 16
EXPERT_FF = 2048
CAPACITY_FACTOR = 2
EPS = 1e-6

LA_QK = LA_HEADS * LA_DK
LA_V = LA_HEADS * LA_DV
ATT_Q = ATT_Q_HEADS * ATT_HD
ATT_KV = ATT_KV_HEADS * ATT_HD
CONV_CH = 2 * LA_QK + LA_V
ATT_COLS = ATT_Q + 2 * ATT_KV

LANES = 128
SUBLANES = 8
BF16_ROWS = 16
VMEM_LIMIT = 56 * 1024 * 1024

DELTA_CHUNK = 128
ATT_BLOCK = 128
ATT_QB = 2
PROJ_TM = 1024
PREP_TM = 256
MERGE_TM = 512
DISPATCH_TT = 256
DISPATCH_SUB = 4
DISPATCH_EP = 2
DISPATCH_W = DISPATCH_TT + BF16_ROWS
DISPATCH_WS = 64
COMBINE_TT = 128
COMBINE_W = 256
COMBINE_WS = 64
FFN_TM = 512
SELECT_BLK = 128

_BF16 = jnp.bfloat16
_F32 = jnp.float32


def _params(*sem):
    return pltpu.CompilerParams(dimension_semantics=sem, vmem_limit_bytes=VMEM_LIMIT)


def _rms(x, g):
    return x * lax.rsqrt(jnp.mean(x * x, axis=-1, keepdims=True) + EPS) * g


def _sigmoid(x):
    return 1.0 / (1.0 + jnp.exp(-x))


def _mm(a, b):
    return jnp.dot(a, b, preferred_element_type=_F32)


def _mm_nt(a, b):
    return lax.dot_general(a, b, (((1,), (1,)), ((), ())), preferred_element_type=_F32)


def _mm_tn(a, b):
    return lax.dot_general(a, b, (((0,), (0,)), ((), ())), preferred_element_type=_F32)


def _split3(x):
    h = x.astype(_BF16)
    r = x - h.astype(_F32)
    m = r.astype(_BF16)
    l = (r - m.astype(_F32)).astype(_BF16)
    return h, m, l


def _mm_exact_lhs(a01, x):
    h, m, l = _split3(x)
    return _mm(a01, h) + _mm(a01, m) + _mm(a01, l)


def _proj_body(x_ref, g_ref, w_ref, *refs, epilogue, n_extra):
    extra, o_ref, hn_ref = refs[:n_extra], refs[n_extra], refs[n_extra + 1]

    @pl.when(pl.program_id(1) == 0)
    def _():
        hn_ref[...] = _rms(x_ref[...], g_ref[...]).astype(_BF16)

    epilogue(_mm(hn_ref[...], w_ref[...]), o_ref, *extra)


def _ep_store(acc, o_ref):
    o_ref[...] = acc.astype(o_ref.dtype)


def _ep_sigmoid(acc, o_ref):
    o_ref[...] = _sigmoid(acc).astype(o_ref.dtype)


def _ep_beta_decay(acc, o_ref, a_ref, dt_ref):
    z = acc + dt_ref[...]
    sp = jnp.maximum(z, 0.0) + jnp.log1p(jnp.exp(-jnp.abs(z)))
    lane = lax.broadcasted_iota(jnp.int32, acc.shape, 1)
    g = jnp.where(lane < 4 * LA_HEADS, -jnp.exp(a_ref[...]) * sp, 0.0)
    o_ref[...] = jnp.where(lane < 2 * LA_HEADS, _sigmoid(acc), g)


def _ep_rope(acc, o_ref, cos_ref, sin_ref):
    c, s = cos_ref[...], sin_ref[...]
    for h in range(ATT_Q_HEADS + ATT_KV_HEADS):
        sl = slice(h * ATT_HD, (h + 1) * ATT_HD)
        x = acc[:, sl]
        o_ref[:, sl] = (x * c + pltpu.roll(x, ATT_HD // 2, 1) * s).astype(o_ref.dtype)
    o_ref[:, ATT_Q + ATT_KV:] = acc[:, ATT_Q + ATT_KV:].astype(o_ref.dtype)


def _project(x, g, w, out_dtype, epilogue, tn, extras=(), extra_specs=()):
    t, d = x.shape
    n = w.shape[1]
    tm = min(PROJ_TM, t)
    body = functools.partial(_proj_body, epilogue=epilogue, n_extra=len(extras))
    return pl.pallas_call(
        body,
        grid=(t // tm, n // tn),
        in_specs=[pl.BlockSpec((tm, d), lambda i, j: (i, 0)),
                  pl.BlockSpec((1, d), lambda i, j: (0, 0)),
                  pl.BlockSpec((d, tn), lambda i, j: (0, j)),
                  *extra_specs],
        out_specs=pl.BlockSpec((tm, tn), lambda i, j: (i, j)),
        out_shape=jax.ShapeDtypeStruct((t, n), out_dtype),
        scratch_shapes=[pltpu.VMEM((tm, d), _BF16)],
        compiler_params=_params("parallel", "arbitrary"),
        name="in_proj",
    )(x, g, w, *extras)


def _prep_body(cur_ref, prev_ref, next_ref, w_ref, q_ref, k_ref, v_ref, ext_ref, *, tiles_per_seq, tm):
    i = pl.program_id(0)
    pos = lax.rem(i, tiles_per_seq)
    keep_prev = jnp.where(pos == 0, 0.0, 1.0)
    keep_next = jnp.where(pos == tiles_per_seq - 1, 0.0, 1.0)
    hr = BF16_ROWS
    for c, o_ref in enumerate((q_ref, k_ref, v_ref)):
        sl = slice(c * LA_QK, (c + 1) * LA_QK)
        ext_ref[0:hr, :] = prev_ref[:, sl].astype(_F32) * keep_prev
        ext_ref[hr:hr + tm, :] = cur_ref[:, sl].astype(_F32)
        ext_ref[hr + tm:, :] = next_ref[:, sl].astype(_F32) * keep_next
        w = w_ref[:, sl]
        y = (ext_ref[hr - 2:hr - 2 + tm, :] * w[0:1]
             + ext_ref[hr - 1:hr - 1 + tm, :] * w[1:2]
             + ext_ref[hr:hr + tm, :] * w[2:3]
             + ext_ref[hr + 1:hr + 1 + tm, :] * w[3:4])
        y = y * _sigmoid(y)
        if c == 2:
            o_ref[...] = y.astype(o_ref.dtype)
            continue
        scale = LA_DK ** -0.5 if c == 0 else 1.0
        for h in range(LA_HEADS):
            hs = slice(h * LA_DK, (h + 1) * LA_DK)
            yh = y[:, hs]
            yh = yh * lax.rsqrt(jnp.sum(yh * yh, axis=-1, keepdims=True) + EPS)
            o_ref[:, hs] = (yh * scale).astype(o_ref.dtype)


def _delta_prep(qkv_raw, conv_w, seq):
    t = qkv_raw.shape[0]
    tm = min(PREP_TM, seq)
    halo_blocks = tm // BF16_ROWS
    n_halo = t // BF16_ROWS
    body = functools.partial(_prep_body, tiles_per_seq=seq // tm, tm=tm)
    out = jax.ShapeDtypeStruct((t, LA_QK), _BF16)
    return pl.pallas_call(
        body,
        grid=(t // tm,),
        in_specs=[pl.BlockSpec((tm, CONV_CH), lambda i: (i, 0)),
                  pl.BlockSpec((BF16_ROWS, CONV_CH), lambda i: (jnp.maximum(i * halo_blocks - 1, 0), 0)),
                  pl.BlockSpec((BF16_ROWS, CONV_CH), lambda i: (jnp.minimum((i + 1) * halo_blocks, n_halo - 1), 0)),
                  pl.BlockSpec((CONV_W, CONV_CH), lambda i: (0, 0))],
        out_specs=[pl.BlockSpec((tm, LA_QK), lambda i: (i, 0))] * 3,
        out_shape=[out, out, out],
        scratch_shapes=[pltpu.VMEM((tm + 2 * BF16_ROWS, LA_QK), _F32)],
        compiler_params=_params("parallel"),
        name="delta_prep",
    )(qkv_raw, qkv_raw, qkv_raw, conv_w)


def _delta_body(qf, kf, vf, bgf, gtf, qb, kb, vb, bgb, gtb, of_ref, ob_ref, state_ref):
    @pl.when(pl.program_id(1) == 0)
    def _():
        state_ref[...] = jnp.zeros_like(state_ref)

    c = DELTA_CHUNK
    row = lax.broadcasted_iota(jnp.int32, (c, c), 0)
    col = lax.broadcasted_iota(jnp.int32, (c, c), 1)
    eye = jnp.where(row == col, 1.0, 0.0)
    refs = ((qf, kf, vf, bgf, gtf, of_ref), (qb, kb, vb, bgb, gtb, ob_ref))

    chains = []
    for d, (q_ref, k_ref, v_ref, bg_ref, gt_ref, o_ref) in enumerate(refs):
        incl = (row >= col) if d == 0 else (row <= col)
        strict = (row > col) if d == 0 else (row < col)
        incl_b = jnp.where(incl, 1.0, 0.0).astype(_BF16)
        incl_t = jnp.where((row <= col) if d == 0 else (row >= col), 1.0, 0.0).astype(_BF16)
        bg = bg_ref[...]
        gc_cols = _mm_exact_lhs(incl_b, bg)
        h3 = _split3(gt_ref[0])
        gc_rows = _mm(h3[0], incl_t) + _mm(h3[1], incl_t) + _mm(h3[2], incl_t)
        total_row = c - 1 if d == 0 else 0
        for h in range(LA_HEADS):
            bcol = LA_HEADS * d + h
            gcol = 2 * LA_HEADS + LA_HEADS * d + h
            chains.append(dict(
                d=d, h=h, hs=slice(h * LA_DK, (h + 1) * LA_DK), incl=incl, strict=strict,
                q_ref=q_ref, k_ref=k_ref, v_ref=v_ref, o_ref=o_ref,
                beta=bg[:, bcol:bcol + 1], gc=gc_cols[:, gcol:gcol + 1], gc_r=gc_rows[gcol:gcol + 1, :],
                g_tot=gc_cols[total_row:total_row + 1, gcol:gcol + 1]))

    for ch in chains:
        q = ch["q_ref"][:, ch["hs"]].astype(_F32)
        k = ch["k_ref"][:, ch["hs"]]
        kf32 = k.astype(_F32)
        ch["kb"] = kf32 * ch["beta"]
        ch["egc"] = jnp.exp(ch["gc"])
        ch["a_qk"] = _mm_nt(jnp.concatenate([ch["kb"], q], axis=0).astype(_BF16), k)
        ch["qg"] = (q * ch["egc"]).astype(_BF16)
        ch["k_dec"] = (kf32 * jnp.exp(ch["g_tot"] - ch["gc"])).astype(_BF16)

    for ch in chains:
        incl = ch["incl"]
        decay = jnp.where(incl, jnp.exp(jnp.where(incl, ch["gc"] - ch["gc_r"], 0.0)), 0.0)
        l_mat = jnp.where(ch["strict"], ch["a_qk"][:c] * decay, 0.0)
        ch["qk"] = (ch["a_qk"][c:] * decay).astype(_BF16)
        ch["y"] = eye - l_mat
        l_b = l_mat.astype(_BF16)
        ch["m_b"] = _mm(l_b, l_b).astype(_BF16)
    p = 2
    while p < c:
        last = 2 * p >= c
        for ch in chains:
            lhs = ch["y"].astype(_BF16) if last else jnp.concatenate([ch["m_b"], ch["y"].astype(_BF16)], axis=0)
            prod = _mm(lhs, ch["m_b"])
            if last:
                ch["y"] = ch["y"] + prod
            else:
                ch["m_b"] = prod[:c].astype(_BF16)
                ch["y"] = ch["y"] + prod[c:]
        p *= 2

    for ch in chains:
        v = ch["v_ref"][:, ch["hs"]].astype(_F32)
        rhs = jnp.concatenate([v * ch["beta"], ch["kb"] * ch["egc"]], axis=1).astype(_BF16)
        ch["uw"] = _mm(ch["y"].astype(_BF16), rhs)
    for ch in chains:
        ch["state"] = state_ref[ch["d"], ch["h"]]
        w_b = ch["uw"][:, LA_DV:].astype(_BF16)
        ch["ws_qs"] = _mm(jnp.concatenate([w_b, ch["qg"]], axis=0), ch["state"].astype(_BF16))
    for ch in chains:
        v_new_b = (ch["uw"][:, :LA_DV] - ch["ws_qs"][:c]).astype(_BF16)
        ch["o_ref"][:, ch["hs"]] = ch["ws_qs"][c:] + _mm(ch["qk"], v_new_b)
        state_ref[ch["d"], ch["h"]] = ch["state"] * jnp.exp(ch["g_tot"]) + _mm_tn(ch["k_dec"], v_new_b)


def _delta_rule(q, k, v, bg, g_t, batch, seq):
    t = q.shape[0]
    c = DELTA_CHUNK
    n = seq // c
    fwd = lambda b, j: (b * n + j, 0)
    bwd = lambda b, j: (b * n + n - 1 - j, 0)
    fwd3 = lambda b, j: (b * n + j, 0, 0)
    bwd3 = lambda b, j: (b * n + n - 1 - j, 0, 0)

    def specs(im, im3):
        return [pl.BlockSpec((c, LA_QK), im), pl.BlockSpec((c, LA_QK), im), pl.BlockSpec((c, LA_V), im),
                pl.BlockSpec((c, LANES), im), pl.BlockSpec((1, 4 * LA_HEADS, c), im3)]

    out = jax.ShapeDtypeStruct((t, LA_V), _F32)
    return pl.pallas_call(
        _delta_body,
        grid=(batch, n),
        in_specs=specs(fwd, fwd3) + specs(bwd, bwd3),
        out_specs=[pl.BlockSpec((c, LA_V), fwd), pl.BlockSpec((c, LA_V), bwd)],
        out_shape=[out, out],
        scratch_shapes=[pltpu.VMEM((2, LA_HEADS, LA_DK, LA_DV), _F32)],
        compiler_params=_params("parallel", "arbitrary"),
        name="delta_rule",
    )(q, k, v, bg, g_t, q, k, v, bg, g_t)


def _attn_body(sink_ref, q_ref, kp_ref, kc_ref, kn_ref, vp_ref, vc_ref, vn_ref, o_ref, *, seq):
    n = pl.program_id(1)
    blk = ATT_BLOCK
    group = ATT_Q_HEADS // ATT_KV_HEADS
    kw = jnp.concatenate([kp_ref[...], kc_ref[...], kn_ref[...]], axis=0)
    vw = jnp.concatenate([vp_ref[...], vc_ref[...], vn_ref[...]], axis=0)
    row = lax.broadcasted_iota(jnp.int32, (blk, 3 * blk), 0)
    col = lax.broadcasted_iota(jnp.int32, (blk, 3 * blk), 1)
    band = jnp.abs(row + blk - col) <= WINDOW

    chains = []
    for qb in range(ATT_QB):
        kpos = (ATT_QB * n + qb - 1) * blk + col
        valid = jnp.concatenate([band & (kpos >= 0) & (kpos < seq)] * group, axis=0)
        for g in range(ATT_KV_HEADS):
            heads = [g * group + j for j in range(group)]
            chains.append(dict(qb=qb, g=g, heads=heads, valid=valid, rows=slice(qb * blk, (qb + 1) * blk),
                               keys=slice(qb * blk, (qb + 3) * blk), cols=slice(g * ATT_HD, (g + 1) * ATT_HD)))
    for ch in chains:
        qs = jnp.concatenate([q_ref[ch["rows"], h * ATT_HD:(h + 1) * ATT_HD] for h in ch["heads"]], axis=0)
        ch["s"] = _mm_nt(qs, kw[ch["keys"], ch["cols"]]) * (ATT_HD ** -0.5)
    for ch in chains:
        valid, s = ch["valid"], ch["s"]
        sk = jnp.concatenate([jnp.full((blk, 1), sink_ref[h], _F32) for h in ch["heads"]], axis=0)
        m = jnp.maximum(jnp.max(jnp.where(valid, s, -1e30), axis=-1, keepdims=True), sk)
        p = jnp.where(valid, jnp.exp(jnp.where(valid, s, 0.0) - m), 0.0)
        ch["denom"] = jnp.sum(p, axis=-1, keepdims=True) + jnp.exp(sk - m)
        ch["p"] = p.astype(_BF16)
    for ch in chains:
        o = _mm(ch["p"], vw[ch["keys"], ch["cols"]]) / ch["denom"]
        for j, h in enumerate(ch["heads"]):
            o_ref[ch["rows"], h * ATT_HD:(h + 1) * ATT_HD] = o[j * blk:(j + 1) * blk].astype(o_ref.dtype)


def _window_attention(aqkv, sink, batch, seq):
    t = aqkv.shape[0]
    blk = ATT_BLOCK
    nb = seq // blk
    ns = nb // ATT_QB
    kcol = ATT_Q // ATT_KV
    prev = lambda b, n: b * nb + jnp.maximum(ATT_QB * n - 1, 0)
    nxt = lambda b, n: b * nb + jnp.minimum(ATT_QB * n + ATT_QB, nb - 1)
    cur = lambda b, n: b * ns + n
    kv_specs = []
    for c in (kcol, kcol + 1):
        kv_specs += [pl.BlockSpec((blk, ATT_KV), functools.partial(lambda b, n, c: (prev(b, n), c), c=c)),
                     pl.BlockSpec((ATT_QB * blk, ATT_KV), functools.partial(lambda b, n, c: (cur(b, n), c), c=c)),
                     pl.BlockSpec((blk, ATT_KV), functools.partial(lambda b, n, c: (nxt(b, n), c), c=c))]
    return pl.pallas_call(
        functools.partial(_attn_body, seq=seq),
        grid=(batch, ns),
        in_specs=[pl.BlockSpec(memory_space=pltpu.SMEM),
                  pl.BlockSpec((ATT_QB * blk, ATT_Q), lambda b, n: (cur(b, n), 0)),
                  *kv_specs],
        out_specs=pl.BlockSpec((ATT_QB * blk, ATT_Q), lambda b, n: (cur(b, n), 0)),
        out_shape=jax.ShapeDtypeStruct((t, ATT_Q), _BF16),
        compiler_params=_params("parallel", "parallel"),
        name="window_attention",
    )(sink, *([aqkv] * 7))


def _merge_body(of_ref, ob_ref, z_ref, ga_ref, gb_ref, oatt_ref, x_ref, wa_ref, wb_ref, wo_ref,
                ng_ref, fg_ref, wr_ref, xmid_ref, hn_ref, aff_ref, oa_ref):
    o = of_ref[...] + ob_ref[...]
    z = z_ref[...].astype(_F32)
    gate = z * _sigmoid(z)
    for h in range(LA_HEADS):
        hs = slice(h * LA_DV, (h + 1) * LA_DV)
        oh = o[:, hs]
        oh = _rms(oh, ng_ref[...])
        oa_ref[:, hs] = (oh * gate[:, hs]).astype(_BF16)
    merged = (ga_ref[...].astype(_F32) * _mm(oa_ref[...], wa_ref[...])
              + gb_ref[...].astype(_F32) * _mm(oatt_ref[...], wb_ref[...]))
    xmid = x_ref[...] + _mm(merged.astype(_BF16), wo_ref[...])
    xmid_ref[...] = xmid
    hn = _rms(xmid, fg_ref[...])
    hn_ref[...] = hn.astype(_BF16)
    tm = hn.shape[0]
    wh, wm, wl = _split3(wr_ref[...])
    hs3 = jnp.concatenate(_split3(hn), axis=0)
    r1 = _mm(hs3, wh)
    r2 = _mm(hs3[:2 * tm], wm)
    logits = r1[:tm] + r1[tm:2 * tm] + r2[:tm] + r1[2 * tm:] + _mm(hs3[:tm], wl) + r2[tm:]
    e = jnp.exp(logits - jnp.max(logits, axis=-1, keepdims=True))
    aff_ref[...] = e / jnp.sum(e, axis=-1, keepdims=True)


def _merge(o_f, o_b, z, gates, o_att, x, w_a, w_b, w_o, norm_g, ffn_g, w_r):
    t, d = x.shape
    tm = min(MERGE_TM, t)
    row = lambda i: (i, 0)
    fix = lambda i: (0, 0)
    act = pl.BlockSpec((tm, d), row)
    wspec = pl.BlockSpec((d, d), fix)
    return pl.pallas_call(
        _merge_body,
        grid=(t // tm,),
        in_specs=[act, act, act, act, pl.BlockSpec((tm, d), lambda i: (i, 1)), act, act,
                  wspec, wspec, wspec,
                  pl.BlockSpec((1, LA_DV), fix), pl.BlockSpec((1, d), fix),
                  pl.BlockSpec((d, N_EXPERTS), fix)],
        out_specs=[act, act, pl.BlockSpec((tm, N_EXPERTS), row)],
        out_shape=[jax.ShapeDtypeStruct((t, d), _F32), jax.ShapeDtypeStruct((t, d), _BF16),
                   jax.ShapeDtypeStruct((t, N_EXPERTS), _F32)],
        scratch_shapes=[pltpu.VMEM((tm, d), _BF16)],
        compiler_params=_params("parallel"),
        name="merge_router",
    )(o_f, o_b, z, gates, gates, o_att, x, w_a, w_b, w_o, norm_g, ffn_g, w_r)


def _select_body(aff_ref, p_ref, starts_ref, *, cap, n_blk):
    e = N_EXPERTS
    blk = SELECT_BLK

    def bits_of(x):
        return pltpu.bitcast(x, jnp.int32)

    def bisect(it, prefix):
        cand = prefix | jnp.left_shift(jnp.int32(1), 30 - it)
        cnt = jnp.sum(jnp.where(bits_of(aff_ref[...]) >= cand, 1, 0), axis=1, keepdims=True)
        return jnp.where(cnt >= cap, cand, prefix)

    thr = lax.fori_loop(0, 31, bisect, jnp.zeros((e, 1), jnp.int32))
    n_gt = jnp.sum(jnp.where(bits_of(aff_ref[...]) > thr, 1, 0), axis=1, keepdims=True)
    need = (cap - n_gt).astype(_F32)

    r = lax.broadcasted_iota(jnp.int32, (blk, blk), 0)
    c = lax.broadcasted_iota(jnp.int32, (blk, blk), 1)
    upper = jnp.where(r <= c, 1.0, 0.0).astype(_BF16)
    ones = jnp.ones((SUBLANES, blk), _BF16)

    def block(j, carry):
        eq_c, sel_c, sel_row = carry
        off = pl.multiple_of(j * blk, blk)
        b = bits_of(aff_ref[:, pl.ds(off, blk)])
        gt = b > thr
        eq = b == thr
        eq_b = jnp.where(eq, 1.0, 0.0).astype(_BF16)
        eq_incl = _mm(eq_b, upper) + eq_c
        sel = gt | (eq & (eq_incl <= need))
        sel_b = jnp.where(sel, 1.0, 0.0).astype(_BF16)
        sel_incl = _mm(sel_b, upper) + sel_c
        p_ref[:, pl.ds(off, blk)] = jnp.where(sel, sel_incl, 0.0).astype(jnp.int32)
        starts_ref[pl.ds(j, 1), :] = sel_row[0:1].astype(jnp.int32)
        return (eq_c + jnp.sum(eq_b.astype(_F32), axis=1, keepdims=True),
                sel_c + jnp.sum(sel_b.astype(_F32), axis=1, keepdims=True),
                sel_row + _mm_nt(ones, sel_b))

    zero = jnp.zeros((e, 1), _F32)
    _, _, totals = lax.fori_loop(0, n_blk, block, (zero, zero, jnp.zeros((SUBLANES, e), _F32)))
    starts_ref[n_blk:n_blk + 1, :] = totals[0:1].astype(jnp.int32)


def _select(aff, cap):
    e, t = aff.shape
    n_blk = t // SELECT_BLK
    return pl.pallas_call(
        functools.partial(_select_body, cap=cap, n_blk=n_blk),
        out_shape=[jax.ShapeDtypeStruct((e, t), jnp.int32), jax.ShapeDtypeStruct((n_blk + 1, e), jnp.int32)],
        compiler_params=pltpu.CompilerParams(vmem_limit_bytes=VMEM_LIMIT),
        name="expert_select",
    )(aff)


def _dispatch_body(starts_ref, hn_ref, p_ref, o_ref, x_ref, *, cap, blocks_per_win):
    eb = pl.program_id(0)
    i = pl.program_id(1)
    tt = DISPATCH_TT
    stride = blocks_per_win * N_EXPERTS

    @pl.when(i == 0)
    def _():
        x_ref[...] = jnp.zeros_like(x_ref)

    offs = {}
    all_short = None
    for s in range(DISPATCH_SUB):
        win = i * DISPATCH_SUB + s
        for j in range(DISPATCH_EP):
            e = eb * DISPATCH_EP + j
            start = starts_ref[win * stride + e]
            end = starts_ref[(win + 1) * stride + e]
            a0 = pl.multiple_of((start // SUBLANES) * SUBLANES, SUBLANES)
            offs[s, j] = a0
            short = end - a0 <= DISPATCH_WS
            all_short = short if all_short is None else jnp.logical_and(all_short, short)

    def gather(w):
        slot = lax.broadcasted_iota(jnp.int32, (w, tt), 0)
        for s in range(DISPATCH_SUB):
            onehots = []
            for j in range(DISPATCH_EP):
                e = eb * DISPATCH_EP + j
                rel = p_ref[pl.ds(e, 1), s * tt:(s + 1) * tt] - 1 - offs[s, j]
                onehots.append(jnp.where(rel == slot, 1.0, 0.0).astype(_BF16))
            rows = _mm(jnp.concatenate(onehots, axis=0), hn_ref[s * tt:(s + 1) * tt, :])
            for j in range(DISPATCH_EP):
                x_ref[j, pl.ds(offs[s, j], w), :] += rows[j * w:(j + 1) * w]

    pl.when(all_short)(functools.partial(gather, DISPATCH_WS))
    pl.when(jnp.logical_not(all_short))(functools.partial(gather, DISPATCH_W))

    @pl.when(i == pl.num_programs(1) - 1)
    def _():
        o_ref[...] = x_ref[:, 0:cap, :].astype(o_ref.dtype)


def _dispatch(starts, hn, p, cap):
    t, d = hn.shape
    tb = DISPATCH_TT * DISPATCH_SUB
    ep = DISPATCH_EP
    body = functools.partial(_dispatch_body, cap=cap, blocks_per_win=DISPATCH_TT // SELECT_BLK)
    return pl.pallas_call(
        body,
        grid_spec=pltpu.PrefetchScalarGridSpec(
            num_scalar_prefetch=1,
            grid=(N_EXPERTS // ep, t // tb),
            in_specs=[pl.BlockSpec((tb, d), lambda e, i, s: (i, 0)),
                      pl.BlockSpec((N_EXPERTS, tb), lambda e, i, s: (0, i))],
            out_specs=pl.BlockSpec((ep, cap, d), lambda e, i, s: (e, 0, 0)),
            scratch_shapes=[pltpu.VMEM((ep, cap + DISPATCH_W, d), _F32)]),
        out_shape=jax.ShapeDtypeStruct((N_EXPERTS, cap, d), _BF16),
        compiler_params=_params("parallel", "arbitrary"),
        name="expert_dispatch",
    )(starts, hn, p)


def _ffn_body(x_ref, wg_ref, wu_ref, wd_ref, o_ref):
    x = x_ref[0]
    a = _mm(x, wg_ref[0])
    b = _mm(x, wu_ref[0])
    hid = (a * _sigmoid(a) * b).astype(_BF16)
    o_ref[0] = _mm(hid, wd_ref[0]).astype(o_ref.dtype)


def _expert_ffn(xe, w_gate, w_up, w_down):
    e, cap, d = xe.shape
    ff = w_gate.shape[2]
    tm = min(FFN_TM, cap)
    return pl.pallas_call(
        _ffn_body,
        grid=(e, cap // tm),
        in_specs=[pl.BlockSpec((1, tm, d), lambda e, i: (e, i, 0)),
                  pl.BlockSpec((1, d, ff), lambda e, i: (e, 0, 0)),
                  pl.BlockSpec((1, d, ff), lambda e, i: (e, 0, 0)),
                  pl.BlockSpec((1, ff, d), lambda e, i: (e, 0, 0))],
        out_specs=pl.BlockSpec((1, tm, d), lambda e, i: (e, i, 0)),
        out_shape=jax.ShapeDtypeStruct((e, cap, d), _BF16),
        compiler_params=_params("parallel", "arbitrary"),
        name="expert_ffn",
    )(xe, w_gate, w_up, w_down)


def _combine_body(starts_ref, xmid_ref, pt_ref, afft_ref, g_ref, ye_ref, o_ref, sbuf_ref, bbuf_ref, sem_ref,
                  *, cap, final_norm):
    i = pl.program_id(0)
    n = pl.num_programs(0)
    tt = COMBINE_TT

    def windows(tile, slot):
        per_expert = []
        all_short = None
        for e in range(N_EXPERTS):
            start = starts_ref[tile * N_EXPERTS + e]
            end = starts_ref[(tile + 1) * N_EXPERTS + e]
            aligned = (start // BF16_ROWS) * BF16_ROWS
            pair = []
            for w, buf in ((COMBINE_WS, sbuf_ref), (COMBINE_W, bbuf_ref)):
                a0 = pl.multiple_of(jnp.minimum(aligned, cap - w), BF16_ROWS)
                copy = pltpu.make_async_copy(ye_ref.at[e, pl.ds(a0, w), :], buf.at[slot, pl.ds(e * w, w), :],
                                             sem_ref.at[slot, e])
                pair.append((a0, copy))
            per_expert.append(pair)
            short = end - pair[0][0] <= COMBINE_WS
            all_short = short if all_short is None else jnp.logical_and(all_short, short)
        return all_short, per_expert

    def start_tile(tile, slot):
        all_short, per_expert = windows(tile, slot)
        for k in range(2):
            @pl.when(all_short if k == 0 else jnp.logical_not(all_short))
            def _():
                for pair in per_expert:
                    pair[k][1].start()

    @pl.when(i == 0)
    def _():
        start_tile(0, 0)

    @pl.when(i + 1 < n)
    def _():
        start_tile(i + 1, lax.rem(i + 1, 2))

    slot = lax.rem(i, 2)
    all_short, per_expert = windows(i, slot)

    def expand(k, w, buf):
        group = max(1, LANES // w)
        width = group * w
        lane = lax.broadcasted_iota(jnp.int32, (tt, width), 1)
        his, los = [], []
        for e0 in range(0, N_EXPERTS, group):
            rel = gate = None
            for j in range(group):
                e = e0 + j
                per_expert[e][k][1].wait()
                rel_e = pt_ref[:, e:e + 1] - 1 - per_expert[e][k][0]
                gate_e = afft_ref[:, e:e + 1]
                rel = rel_e if j == 0 else jnp.where(lane >= j * w, rel_e, rel)
                gate = gate_e if j == 0 else jnp.where(lane >= j * w, gate_e, gate)
            piece = jnp.where(rel == (lane & (w - 1)), gate, 0.0)
            hi = piece.astype(_BF16)
            his.append(hi)
            los.append((piece - hi.astype(_F32)).astype(_BF16))
        lhs = jnp.concatenate([jnp.concatenate(his, axis=1), jnp.concatenate(los, axis=1)], axis=0)
        res = _mm(lhs, buf[slot])
        acc = xmid_ref[...] + res[:tt] + res[tt:]
        o_ref[...] = _rms(acc, g_ref[...]) if final_norm else acc

    pl.when(all_short)(functools.partial(expand, 0, COMBINE_WS, sbuf_ref))
    pl.when(jnp.logical_not(all_short))(functools.partial(expand, 1, COMBINE_W, bbuf_ref))


def _combine(starts, xmid, p_t, aff_t, g_final, ye, cap, final_norm):
    t, d = xmid.shape
    tt = COMBINE_TT
    return pl.pallas_call(
        functools.partial(_combine_body, cap=cap, final_norm=final_norm),
        grid_spec=pltpu.PrefetchScalarGridSpec(
            num_scalar_prefetch=1,
            grid=(t // tt,),
            in_specs=[pl.BlockSpec((tt, d), lambda i, s: (i, 0)),
                      pl.BlockSpec((tt, N_EXPERTS), lambda i, s: (i, 0)),
                      pl.BlockSpec((tt, N_EXPERTS), lambda i, s: (i, 0)),
                      pl.BlockSpec((1, d), lambda i, s: (0, 0)),
                      pl.BlockSpec(memory_space=pl.ANY)],
            out_specs=pl.BlockSpec((tt, d), lambda i, s: (i, 0)),
            scratch_shapes=[pltpu.VMEM((2, N_EXPERTS * COMBINE_WS, d), _BF16),
                            pltpu.VMEM((2, N_EXPERTS * COMBINE_W, d), _BF16),
                            pltpu.SemaphoreType.DMA((2, N_EXPERTS))]),
        out_shape=jax.ShapeDtypeStruct((t, d), _F32),
        compiler_params=_params("arbitrary"),
        name="expert_combine",
    )(starts, xmid, p_t, aff_t, g_final, ye)


def _rope_tables(seq):
    half = ATT_HD // 2
    inv_freq = ROPE_THETA ** (-jnp.arange(half, dtype=_F32) / half)
    ang = jnp.arange(seq, dtype=_F32)[:, None] * inv_freq[None, :]
    cos, sin = jnp.cos(ang), jnp.sin(ang)
    return jnp.concatenate([cos, cos], axis=-1), jnp.concatenate([-sin, sin], axis=-1)


def _layer(x, batch, seq, w, final_g, final_norm):
    t, d = x.shape
    tm = min(PROJ_TM, t)
    g_mix = w["norm_mix_g"]
    qkv_raw = _project(x, g_mix, w["w_conv"], _BF16, _ep_store, 1024)
    z = _project(x, g_mix, w["w_z"], _BF16, _ep_store, 1024)
    fix = pl.BlockSpec((1, LANES), lambda i, j: (0, 0))
    bg = _project(x, g_mix, w["w_bg"], _F32, _ep_beta_decay, LANES,
                  extras=(w["a_log"], w["dt_bias"]), extra_specs=(fix, fix))
    rope_spec = pl.BlockSpec((tm, ATT_HD), lambda i, j: (i % (seq // tm), 0))
    aqkv = _project(x, g_mix, w["w_att"], _BF16, _ep_rope, ATT_COLS,
                    extras=w["rope"], extra_specs=(rope_spec, rope_spec))
    gates = _project(x, g_mix, w["w_gates"], _BF16, _ep_sigmoid, 1024)

    q, k, v = _delta_prep(qkv_raw, w["conv_w"], seq)
    g_t = bg[:, :4 * LA_HEADS].reshape(t // DELTA_CHUNK, DELTA_CHUNK, 4 * LA_HEADS).transpose(0, 2, 1)
    o_f, o_b = _delta_rule(q, k, v, bg, g_t, batch, seq)
    o_att = _window_attention(aqkv, w["sink"], batch, seq)
    xmid, hn, aff_t = _merge(o_f, o_b, z, gates, o_att, x, w["w_a"], w["w_b"], w["w_o"],
                             w["la_norm_g"], w["norm_ffn_g"], w["w_r"])

    cap = max(1, CAPACITY_FACTOR * t // N_EXPERTS)
    assert cap >= COMBINE_W and cap % min(FFN_TM, cap) == 0 and cap % BF16_ROWS == 0
    p, starts = _select(aff_t.T, cap)
    starts = starts.reshape(-1)
    xe = _dispatch(starts, hn, p, cap)
    ye = _expert_ffn(xe, w["w_gate"], w["w_up"], w["w_down"])
    return _combine(starts, xmid, p.T, aff_t, final_g, ye, cap, final_norm)


def _layer_weights(l, seq, norm_mix_g, w_in, conv_w, la_a_log, la_dt_bias, la_norm_g, attn_sink, w_branch_a,
                   w_branch_b, w_out, norm_ffn_g, w_router, w_exp_gate, w_exp_up, w_exp_down):
    wi = w_in[l]
    c0 = CONV_CH
    c1 = c0 + LA_V
    c2 = c1 + 4 * LA_HEADS
    c3 = c2 + ATT_COLS
    pad = LANES - 4 * LA_HEADS
    zeros = jnp.zeros((2 * LA_HEADS,), _F32)
    lane_pad = lambda v: jnp.pad(jnp.concatenate([zeros, v.reshape(-1)]), (0, pad)).reshape(1, LANES)
    return {
        "norm_mix_g": norm_mix_g[l].reshape(1, -1),
        "w_conv": wi[:, :c0].astype(_BF16),
        "w_z": wi[:, c0:c1].astype(_BF16),
        "w_bg": jnp.pad(wi[:, c1:c2], ((0, 0), (0, pad))).astype(_BF16),
        "w_att": wi[:, c2:c3].astype(_BF16),
        "w_gates": wi[:, c3:].astype(_BF16),
        "a_log": lane_pad(la_a_log[l]),
        "dt_bias": lane_pad(la_dt_bias[l]),
        "rope": _rope_tables(seq),
        "conv_w": conv_w[l],
        "sink": attn_sink[l],
        "w_a": w_branch_a[l].astype(_BF16),
        "w_b": w_branch_b[l].astype(_BF16),
        "w_o": w_out[l].astype(_BF16),
        "la_norm_g": la_norm_g[l].reshape(1, -1),
        "norm_ffn_g": norm_ffn_g[l].reshape(1, -1),
        "w_r": w_router[l],
        "w_gate": w_exp_gate[l].astype(_BF16),
        "w_up": w_exp_up[l].astype(_BF16),
        "w_down": w_exp_down[l].astype(_BF16),
    }


def kernel(x_prompt, x_sample, norm_mix_g, w_in, conv_w, la_a_log, la_dt_bias, la_norm_g, attn_sink, w_branch_a, w_branch_b, w_out, norm_ffn_g, w_router, w_exp_gate, w_exp_up, w_exp_down, final_norm_g):
    depth = w_in.shape[0]
    layer_args = (norm_mix_g, w_in, conv_w, la_a_log, la_dt_bias, la_norm_g, attn_sink, w_branch_a, w_branch_b,
                  w_out, norm_ffn_g, w_router, w_exp_gate, w_exp_up, w_exp_down)
    final_g = final_norm_g.reshape(1, -1)

    def trunk(x):
        batch, seq, d = x.shape
        h = x.reshape(batch * seq, d)
        for l in range(depth):
            w = _layer_weights(l, seq, *layer_args)
            h = _layer(h, batch, seq, w, final_g, l == depth - 1)
        return h.reshape(batch, seq, d)

    return trunk(x_prompt), trunk(x_sample)
```

```python
import functools

import jax
import jax.numpy as jnp
from jax import lax
from jax.experimental import pallas as pl
from jax.experimental.pallas import tpu as pltpu

D_MODEL = 1024
LA_HEADS = 8
LA_DK = 128
LA_DV = 128
CONV_W = 4
ATT_Q_HEADS = 8
ATT_KV_HEADS = 2
ATT_HD = 128
WINDOW = 128
ROPE_THETA = 10000.0
N_EXPERTS = 16
EXPERT_FF = 2048
CAPACITY_FACTOR = 2
EPS = 1e-6

LA_QK = LA_HEADS * LA_DK
LA_V = LA_HEADS * LA_DV
ATT_Q = ATT_Q_HEADS * ATT_HD
ATT_KV = ATT_KV_HEADS * ATT_HD
CONV_CH = 2 * LA_QK + LA_V
ATT_COLS = ATT_Q + 2 * ATT_KV

LANES = 128
SUBLANES = 8
BF16_ROWS = 16
VMEM_LIMIT = 56 * 1024 * 1024

DELTA_CHUNK = 128
ATT_BLOCK = 128
ATT_QB = 2
PROJ_TM = 512
MERGE_TM = 512
DISPATCH_TT = 256
DISPATCH_SUB = 4
DISPATCH_EP = 2
DISPATCH_W = DISPATCH_TT + BF16_ROWS
DISPATCH_WS = 64
COMBINE_TT = 128
COMBINE_W = 256
COMBINE_WS = 64
FFN_TM = 1024
FFN_CHUNK = 512
SELECT_BLK = 128

_BF16 = jnp.bfloat16
_F32 = jnp.float32


def _params(*sem):
    return pltpu.CompilerParams(dimension_semantics=sem, vmem_limit_bytes=VMEM_LIMIT)


def _rms(x, g):
    return x * lax.rsqrt(jnp.mean(x * x, axis=-1, keepdims=True) + EPS) * g


def _sigmoid(x):
    return 1.0 / (1.0 + jnp.exp(-x))


def _mm(a, b):
    return jnp.dot(a, b, preferred_element_type=_F32)


def _mm_nt(a, b):
    return lax.dot_general(a, b, (((1,), (1,)), ((), ())), preferred_element_type=_F32)


def _mm_tn(a, b):
    return lax.dot_general(a, b, (((0,), (0,)), ((), ())), preferred_element_type=_F32)


def _split3(x):
    h = x.astype(_BF16)
    r = x - h.astype(_F32)
    m = r.astype(_BF16)
    l = (r - m.astype(_F32)).astype(_BF16)
    return h, m, l


def _mm_exact_lhs(a01, x):
    h, m, l = _split3(x)
    return _mm(a01, h) + _mm(a01, m) + _mm(a01, l)


def _proj_body(x_ref, xp_ref, xn_ref, g_ref, wc_ref, wz_ref, wg_ref, wa_ref, wb_ref, cw_ref, alog_ref, dt_ref,
               cos_ref, sin_ref, q_ref, k_ref, v_ref, z_ref, gates_ref, aqkv_ref, bg_ref, hn_ref, ext_ref,
               *, tiles_per_seq, tm):
    i = pl.program_id(0)
    pos = lax.rem(i, tiles_per_seq)
    keep_prev = jnp.where(pos == 0, 0.0, 1.0)
    keep_next = jnp.where(pos == tiles_per_seq - 1, 0.0, 1.0)
    hr = BF16_ROWS
    g = g_ref[...]
    hn_ref[0:hr, :] = _rms(xp_ref[...], g).astype(_BF16)
    hn_ref[hr:hr + tm, :] = _rms(x_ref[...], g).astype(_BF16)
    hn_ref[hr + tm:, :] = _rms(xn_ref[...], g).astype(_BF16)
    hn = hn_ref[hr:hr + tm, :]

    for c, o_ref in enumerate((q_ref, k_ref, v_ref)):
        sl = slice(c * LA_QK, (c + 1) * LA_QK)
        raw = _mm(hn_ref[...], wc_ref[:, sl])
        ext_ref[0:hr, :] = raw[0:hr] * keep_prev
        ext_ref[hr:hr + tm, :] = raw[hr:hr + tm]
        ext_ref[hr + tm:, :] = raw[hr + tm:] * keep_next
        w = cw_ref[:, sl]
        y = (ext_ref[hr - 2:hr - 2 + tm, :] * w[0:1]
             + ext_ref[hr - 1:hr - 1 + tm, :] * w[1:2]
             + ext_ref[hr:hr + tm, :] * w[2:3]
             + ext_ref[hr + 1:hr + 1 + tm, :] * w[3:4])
        y = y * _sigmoid(y)
        if c == 2:
            o_ref[...] = y.astype(o_ref.dtype)
            continue
        scale = LA_DK ** -0.5 if c == 0 else 1.0
        for h in range(LA_HEADS):
            hs = slice(h * LA_DK, (h + 1) * LA_DK)
            yh = y[:, hs]
            yh = yh * lax.rsqrt(jnp.sum(yh * yh, axis=-1, keepdims=True) + EPS)
            o_ref[:, hs] = (yh * scale).astype(o_ref.dtype)

    z_ref[...] = _mm(hn, wz_ref[...]).astype(z_ref.dtype)
    for c in range(2):
        sl = slice(c * D_MODEL, (c + 1) * D_MODEL)
        gates_ref[:, sl] = _sigmoid(_mm(hn, wg_ref[:, sl])).astype(gates_ref.dtype)

    acc = _mm(hn, wa_ref[...])
    cos, sin = cos_ref[...], sin_ref[...]
    for h in range(ATT_Q_HEADS + ATT_KV_HEADS):
        sl = slice(h * ATT_HD, (h + 1) * ATT_HD)
        xh = acc[:, sl]
        aqkv_ref[:, sl] = (xh * cos + pltpu.roll(xh, ATT_HD // 2, 1) * sin).astype(aqkv_ref.dtype)
    aqkv_ref[:, ATT_Q + ATT_KV:] = acc[:, ATT_Q + ATT_KV:].astype(aqkv_ref.dtype)

    acc = _mm(hn, wb_ref[...])
    zz = acc + dt_ref[...]
    sp = jnp.maximum(zz, 0.0) + jnp.log1p(jnp.exp(-jnp.abs(zz)))
    lane = lax.broadcasted_iota(jnp.int32, acc.shape, 1)
    decay = jnp.where(lane < 4 * LA_HEADS, -jnp.exp(alog_ref[...]) * sp, 0.0)
    bg_ref[...] = jnp.where(lane < 2 * LA_HEADS, _sigmoid(acc), decay)


def _project(x, w, seq):
    t, d = x.shape
    tm = min(PROJ_TM, seq)
    halo_blocks = tm // BF16_ROWS
    n_halo = t // BF16_ROWS
    tiles_per_seq = seq // tm
    row = lambda i: (i, 0)
    fix = lambda i: (0, 0)
    resident = lambda shape: pl.BlockSpec(shape, fix, pipeline_mode=pl.Buffered(1))
    rope_spec = pl.BlockSpec((tm, ATT_HD), lambda i: (lax.rem(i, tiles_per_seq), 0))
    bf = lambda n: jax.ShapeDtypeStruct((t, n), _BF16)
    return pl.pallas_call(
        functools.partial(_proj_body, tiles_per_seq=tiles_per_seq, tm=tm),
        grid=(t // tm,),
        in_specs=[pl.BlockSpec((tm, d), row),
                  pl.BlockSpec((BF16_ROWS, d), lambda i: (jnp.maximum(i * halo_blocks - 1, 0), 0)),
                  pl.BlockSpec((BF16_ROWS, d), lambda i: (jnp.minimum((i + 1) * halo_blocks, n_halo - 1), 0)),
                  resident((1, d)),
                  resident((d, CONV_CH)), resident((d, LA_V)), resident((d, 2 * D_MODEL)),
                  resident((d, ATT_COLS)), resident((d, LANES)),
                  resident((CONV_W, CONV_CH)), resident((1, LANES)), resident((1, LANES)),
                  rope_spec, rope_spec],
        out_specs=[pl.BlockSpec((tm, LA_QK), row), pl.BlockSpec((tm, LA_QK), row), pl.BlockSpec((tm, LA_V), row),
                   pl.BlockSpec((tm, LA_V), row), pl.BlockSpec((tm, 2 * D_MODEL), row),
                   pl.BlockSpec((tm, ATT_COLS), row), pl.BlockSpec((tm, LANES), row)],
        out_shape=[bf(LA_QK), bf(LA_QK), bf(LA_V), bf(LA_V), bf(2 * D_MODEL), bf(ATT_COLS),
                   jax.ShapeDtypeStruct((t, LANES), _F32)],
        scratch_shapes=[pltpu.VMEM((tm + 2 * BF16_ROWS, d), _BF16),
                        pltpu.VMEM((tm + 2 * BF16_ROWS, LA_QK), _F32)],
        compiler_params=_params("parallel"),
        name="in_proj",
    )(x, x, x, w["norm_mix_g"], w["w_conv"], w["w_z"], w["w_gates"], w["w_att"], w["w_bg"],
      w["conv_w"], w["a_log"], w["dt_bias"], *w["rope"])


def _delta_body(qf, kf, vf, bgf, gtf, qb, kb, vb, bgb, gtb, of_ref, ob_ref, state_ref):
    @pl.when(pl.program_id(1) == 0)
    def _():
        state_ref[...] = jnp.zeros_like(state_ref)

    c = DELTA_CHUNK
    row = lax.broadcasted_iota(jnp.int32, (c, c), 0)
    col = lax.broadcasted_iota(jnp.int32, (c, c), 1)
    eye = jnp.where(row == col, 1.0, 0.0)
    refs = ((qf, kf, vf, bgf, gtf, of_ref), (qb, kb, vb, bgb, gtb, ob_ref))

    chains = []
    for d, (q_ref, k_ref, v_ref, bg_ref, gt_ref, o_ref) in enumerate(refs):
        incl = (row >= col) if d == 0 else (row <= col)
        strict = (row > col) if d == 0 else (row < col)
        incl_b = jnp.where(incl, 1.0, 0.0).astype(_BF16)
        incl_t = jnp.where((row <= col) if d == 0 else (row >= col), 1.0, 0.0).astype(_BF16)
        bg = bg_ref[...]
        gc_cols = _mm_exact_lhs(incl_b, bg)
        h3 = _split3(gt_ref[0])
        gc_rows = _mm(h3[0], incl_t) + _mm(h3[1], incl_t) + _mm(h3[2], incl_t)
        total_row = c - 1 if d == 0 else 0
        for h in range(LA_HEADS):
            bcol = LA_HEADS * d + h
            gcol = 2 * LA_HEADS + LA_HEADS * d + h
            chains.append(dict(
                d=d, h=h, hs=slice(h * LA_DK, (h + 1) * LA_DK), incl=incl, strict=strict,
                q_ref=q_ref, k_ref=k_ref, v_ref=v_ref, o_ref=o_ref,
                beta=bg[:, bcol:bcol + 1], gc=gc_cols[:, gcol:gcol + 1], gc_r=gc_rows[gcol:gcol + 1, :],
                g_tot=gc_cols[total_row:total_row + 1, gcol:gcol + 1]))

    for ch in chains:
        q = ch["q_ref"][:, ch["hs"]].astype(_F32)
        k = ch["k_ref"][:, ch["hs"]]
        kf32 = k.astype(_F32)
        ch["kb"] = kf32 * ch["beta"]
        ch["egc"] = jnp.exp(ch["gc"])
        ch["a_qk"] = _mm_nt(jnp.concatenate([ch["kb"], q], axis=0).astype(_BF16), k)
        ch["qg"] = (q * ch["egc"]).astype(_BF16)
        ch["k_dec"] = (kf32 * jnp.exp(ch["g_tot"] - ch["gc"])).astype(_BF16)

    for ch in chains:
        incl = ch["incl"]
        decay = jnp.where(incl, jnp.exp(jnp.where(incl, ch["gc"] - ch["gc_r"], 0.0)), 0.0)
        l_mat = jnp.where(ch["strict"], ch["a_qk"][:c] * decay, 0.0)
        ch["qk"] = (ch["a_qk"][c:] * decay).astype(_BF16)
        ch["y"] = eye - l_mat
        l_b = l_mat.astype(_BF16)
        ch["m_b"] = _mm(l_b, l_b).astype(_BF16)
    p = 2
    while p < c:
        last = 2 * p >= c
        for ch in chains:
            lhs = ch["y"].astype(_BF16) if last else jnp.concatenate([ch["m_b"], ch["y"].astype(_BF16)], axis=0)
            prod = _mm(lhs, ch["m_b"])
            if last:
                ch["y"] = ch["y"] + prod
            else:
                ch["m_b"] = prod[:c].astype(_BF16)
                ch["y"] = ch["y"] + prod[c:]
        p *= 2

    for ch in chains:
        v = ch["v_ref"][:, ch["hs"]].astype(_F32)
        rhs = jnp.concatenate([v * ch["beta"], ch["kb"] * ch["egc"]], axis=1).astype(_BF16)
        ch["uw"] = _mm(ch["y"].astype(_BF16), rhs)
    for ch in chains:
        ch["state"] = state_ref[ch["d"], ch["h"]]
        w_b = ch["uw"][:, LA_DV:].astype(_BF16)
        ch["ws_qs"] = _mm(jnp.concatenate([w_b, ch["qg"]], axis=0), ch["state"].astype(_BF16))
    for ch in chains:
        v_new_b = (ch["uw"][:, :LA_DV] - ch["ws_qs"][:c]).astype(_BF16)
        ch["o_ref"][:, ch["hs"]] = ch["ws_qs"][c:] + _mm(ch["qk"], v_new_b)
        state_ref[ch["d"], ch["h"]] = ch["state"] * jnp.exp(ch["g_tot"]) + _mm_tn(ch["k_dec"], v_new_b)


def _delta_rule(q, k, v, bg, g_t, batch, seq):
    t = q.shape[0]
    c = DELTA_CHUNK
    n = seq // c
    fwd = lambda b, j: (b * n + j, 0)
    bwd = lambda b, j: (b * n + n - 1 - j, 0)
    fwd3 = lambda b, j: (b * n + j, 0, 0)
    bwd3 = lambda b, j: (b * n + n - 1 - j, 0, 0)

    def specs(im, im3):
        return [pl.BlockSpec((c, LA_QK), im), pl.BlockSpec((c, LA_QK), im), pl.BlockSpec((c, LA_V), im),
                pl.BlockSpec((c, LANES), im), pl.BlockSpec((1, 4 * LA_HEADS, c), im3)]

    out = jax.ShapeDtypeStruct((t, LA_V), _F32)
    return pl.pallas_call(
        _delta_body,
        grid=(batch, n),
        in_specs=specs(fwd, fwd3) + specs(bwd, bwd3),
        out_specs=[pl.BlockSpec((c, LA_V), fwd), pl.BlockSpec((c, LA_V), bwd)],
        out_shape=[out, out],
        scratch_shapes=[pltpu.VMEM((2, LA_HEADS, LA_DK, LA_DV), _F32)],
        compiler_params=_params("parallel", "arbitrary"),
        name="delta_rule",
    )(q, k, v, bg, g_t, q, k, v, bg, g_t)


def _attn_body(sink_ref, q_ref, kp_ref, kc_ref, kn_ref, vp_ref, vc_ref, vn_ref, o_ref, *, seq):
    n = pl.program_id(1)
    blk = ATT_BLOCK
    group = ATT_Q_HEADS // ATT_KV_HEADS
    kw = jnp.concatenate([kp_ref[...], kc_ref[...], kn_ref[...]], axis=0)
    vw = jnp.concatenate([vp_ref[...], vc_ref[...], vn_ref[...]], axis=0)
    row = lax.broadcasted_iota(jnp.int32, (blk, 3 * blk), 0)
    col = lax.broadcasted_iota(jnp.int32, (blk, 3 * blk), 1)
    band = jnp.abs(row + blk - col) <= WINDOW

    chains = []
    for qb in range(ATT_QB):
        kpos = (ATT_QB * n + qb - 1) * blk + col
        valid = jnp.concatenate([band & (kpos >= 0) & (kpos < seq)] * group, axis=0)
        for g in range(ATT_KV_HEADS):
            heads = [g * group + j for j in range(group)]
            chains.append(dict(qb=qb, g=g, heads=heads, valid=valid, rows=slice(qb * blk, (qb + 1) * blk),
                               keys=slice(qb * blk, (qb + 3) * blk), cols=slice(g * ATT_HD, (g + 1) * ATT_HD)))
    for ch in chains:
        qs = jnp.concatenate([q_ref[ch["rows"], h * ATT_HD:(h + 1) * ATT_HD] for h in ch["heads"]], axis=0)
        ch["s"] = _mm_nt(qs, kw[ch["keys"], ch["cols"]]) * (ATT_HD ** -0.5)
    for ch in chains:
        valid, s = ch["valid"], ch["s"]
        sk = jnp.concatenate([jnp.full((blk, 1), sink_ref[h], _F32) for h in ch["heads"]], axis=0)
        m = jnp.maximum(jnp.max(jnp.where(valid, s, -1e30), axis=-1, keepdims=True), sk)
        p = jnp.where(valid, jnp.exp(jnp.where(valid, s, 0.0) - m), 0.0)
        ch["denom"] = jnp.sum(p, axis=-1, keepdims=True) + jnp.exp(sk - m)
        ch["p"] = p.astype(_BF16)
    for ch in chains:
        o = _mm(ch["p"], vw[ch["keys"], ch["cols"]]) / ch["denom"]
        for j, h in enumerate(ch["heads"]):
            o_ref[ch["rows"], h * ATT_HD:(h + 1) * ATT_HD] = o[j * blk:(j + 1) * blk].astype(o_ref.dtype)


def _window_attention(aqkv, sink, batch, seq):
    t = aqkv.shape[0]
    blk = ATT_BLOCK
    nb = seq // blk
    ns = nb // ATT_QB
    kcol = ATT_Q // ATT_KV
    prev = lambda b, n: b * nb + jnp.maximum(ATT_QB * n - 1, 0)
    nxt = lambda b, n: b * nb + jnp.minimum(ATT_QB * n + ATT_QB, nb - 1)
    cur = lambda b, n: b * ns + n
    kv_specs = []
    for c in (kcol, kcol + 1):
        kv_specs += [pl.BlockSpec((blk, ATT_KV), functools.partial(lambda b, n, c: (prev(b, n), c), c=c)),
                     pl.BlockSpec((ATT_QB * blk, ATT_KV), functools.partial(lambda b, n, c: (cur(b, n), c), c=c)),
                     pl.BlockSpec((blk, ATT_KV), functools.partial(lambda b, n, c: (nxt(b, n), c), c=c))]
    return pl.pallas_call(
        functools.partial(_attn_body, seq=seq),
        grid=(batch, ns),
        in_specs=[pl.BlockSpec(memory_space=pltpu.SMEM),
                  pl.BlockSpec((ATT_QB * blk, ATT_Q), lambda b, n: (cur(b, n), 0)),
                  *kv_specs],
        out_specs=pl.BlockSpec((ATT_QB * blk, ATT_Q), lambda b, n: (cur(b, n), 0)),
        out_shape=jax.ShapeDtypeStruct((t, ATT_Q), _BF16),
        compiler_params=_params("parallel", "parallel"),
        name="window_attention",
    )(sink, *([aqkv] * 7))


def _merge_body(of_ref, ob_ref, z_ref, ga_ref, gb_ref, oatt_ref, x_ref, wa_ref, wb_ref, wo_ref,
                ng_ref, fg_ref, wr_ref, xmid_ref, hn_ref, aff_ref, oa_ref):
    o = of_ref[...] + ob_ref[...]
    z = z_ref[...].astype(_F32)
    gate = z * _sigmoid(z)
    for h in range(LA_HEADS):
        hs = slice(h * LA_DV, (h + 1) * LA_DV)
        oh = o[:, hs]
        oh = _rms(oh, ng_ref[...])
        oa_ref[:, hs] = (oh * gate[:, hs]).astype(_BF16)
    merged = (ga_ref[...].astype(_F32) * _mm(oa_ref[...], wa_ref[...])
              + gb_ref[...].astype(_F32) * _mm(oatt_ref[...], wb_ref[...]))
    xmid = x_ref[...] + _mm(merged.astype(_BF16), wo_ref[...])
    xmid_ref[...] = xmid
    hn = _rms(xmid, fg_ref[...])
    hn_ref[...] = hn.astype(_BF16)
    tm = hn.shape[0]
    wh, wm, wl = _split3(wr_ref[...])
    hs3 = jnp.concatenate(_split3(hn), axis=0)
    r1 = _mm(hs3, wh)
    r2 = _mm(hs3[:2 * tm], wm)
    logits = r1[:tm] + r1[tm:2 * tm] + r2[:tm] + r1[2 * tm:] + _mm(hs3[:tm], wl) + r2[tm:]
    e = jnp.exp(logits - jnp.max(logits, axis=-1, keepdims=True))
    aff_ref[...] = e / jnp.sum(e, axis=-1, keepdims=True)


def _merge(o_f, o_b, z, gates, o_att, x, w_a, w_b, w_o, norm_g, ffn_g, w_r):
    t, d = x.shape
    tm = min(MERGE_TM, t)
    row = lambda i: (i, 0)
    fix = lambda i: (0, 0)
    act = pl.BlockSpec((tm, d), row)
    wspec = pl.BlockSpec((d, d), fix)
    return pl.pallas_call(
        _merge_body,
        grid=(t // tm,),
        in_specs=[act, act, act, act, pl.BlockSpec((tm, d), lambda i: (i, 1)), act, act,
                  wspec, wspec, wspec,
                  pl.BlockSpec((1, LA_DV), fix), pl.BlockSpec((1, d), fix),
                  pl.BlockSpec((d, N_EXPERTS), fix)],
        out_specs=[act, act, pl.BlockSpec((tm, N_EXPERTS), row)],
        out_shape=[jax.ShapeDtypeStruct((t, d), _F32), jax.ShapeDtypeStruct((t, d), _BF16),
                   jax.ShapeDtypeStruct((t, N_EXPERTS), _F32)],
        scratch_shapes=[pltpu.VMEM((tm, d), _BF16)],
        compiler_params=_params("parallel"),
        name="merge_router",
    )(o_f, o_b, z, gates, gates, o_att, x, w_a, w_b, w_o, norm_g, ffn_g, w_r)


def _select_body(aff_ref, p_ref, starts_ref, *, cap, n_blk):
    e = N_EXPERTS
    blk = SELECT_BLK

    def bits_of(x):
        return pltpu.bitcast(x, jnp.int32)

    def bisect(it, prefix):
        cand = prefix | jnp.left_shift(jnp.int32(1), 30 - it)
        cnt = jnp.sum(jnp.where(bits_of(aff_ref[...]) >= cand, 1, 0), axis=1, keepdims=True)
        return jnp.where(cnt >= cap, cand, prefix)

    thr = lax.fori_loop(0, 31, bisect, jnp.zeros((e, 1), jnp.int32))
    n_gt = jnp.sum(jnp.where(bits_of(aff_ref[...]) > thr, 1, 0), axis=1, keepdims=True)
    need = (cap - n_gt).astype(_F32)

    r = lax.broadcasted_iota(jnp.int32, (blk, blk), 0)
    c = lax.broadcasted_iota(jnp.int32, (blk, blk), 1)
    upper = jnp.where(r <= c, 1.0, 0.0).astype(_BF16)
    ones = jnp.ones((SUBLANES, blk), _BF16)

    def block(j, carry):
        eq_c, sel_c, sel_row = carry
        off = pl.multiple_of(j * blk, blk)
        b = bits_of(aff_ref[:, pl.ds(off, blk)])
        gt = b > thr
        eq = b == thr
        eq_b = jnp.where(eq, 1.0, 0.0).astype(_BF16)
        eq_incl = _mm(eq_b, upper) + eq_c
        sel = gt | (eq & (eq_incl <= need))
        sel_b = jnp.where(sel, 1.0, 0.0).astype(_BF16)
        sel_incl = _mm(sel_b, upper) + sel_c
        p_ref[:, pl.ds(off, blk)] = jnp.where(sel, sel_incl, 0.0).astype(jnp.int32)
        starts_ref[pl.ds(j, 1), :] = sel_row[0:1].astype(jnp.int32)
        return (eq_c + jnp.sum(eq_b.astype(_F32), axis=1, keepdims=True),
                sel_c + jnp.sum(sel_b.astype(_F32), axis=1, keepdims=True),
                sel_row + _mm_nt(ones, sel_b))

    zero = jnp.zeros((e, 1), _F32)
    _, _, totals = lax.fori_loop(0, n_blk, block, (zero, zero, jnp.zeros((SUBLANES, e), _F32)))
    starts_ref[n_blk:n_blk + 1, :] = totals[0:1].astype(jnp.int32)


def _select(aff, cap):
    e, t = aff.shape
    n_blk = t // SELECT_BLK
    return pl.pallas_call(
        functools.partial(_select_body, cap=cap, n_blk=n_blk),
        out_shape=[jax.ShapeDtypeStruct((e, t), jnp.int32), jax.ShapeDtypeStruct((n_blk + 1, e), jnp.int32)],
        compiler_params=pltpu.CompilerParams(vmem_limit_bytes=VMEM_LIMIT),
        name="expert_select",
    )(aff)


def _dispatch_body(starts_ref, hn_ref, p_ref, o_ref, x_ref, *, cap, blocks_per_win):
    eb = pl.program_id(0)
    i = pl.program_id(1)
    tt = DISPATCH_TT
    stride = blocks_per_win * N_EXPERTS

    @pl.when(i == 0)
    def _():
        x_ref[...] = jnp.zeros_like(x_ref)

    offs = {}
    all_short = None
    for s in range(DISPATCH_SUB):
        win = i * DISPATCH_SUB + s
        for j in range(DISPATCH_EP):
            e = eb * DISPATCH_EP + j
            start = starts_ref[win * stride + e]
            end = starts_ref[(win + 1) * stride + e]
            a0 = pl.multiple_of((start // SUBLANES) * SUBLANES, SUBLANES)
            offs[s, j] = a0
            short = end - a0 <= DISPATCH_WS
            all_short = short if all_short is None else jnp.logical_and(all_short, short)

    def gather(w):
        slot = lax.broadcasted_iota(jnp.int32, (w, tt), 0)
        for s in range(DISPATCH_SUB):
            onehots = []
            for j in range(DISPATCH_EP):
                e = eb * DISPATCH_EP + j
                rel = p_ref[pl.ds(e, 1), s * tt:(s + 1) * tt] - 1 - offs[s, j]
                onehots.append(jnp.where(rel == slot, 1.0, 0.0).astype(_BF16))
            rows = _mm(jnp.concatenate(onehots, axis=0), hn_ref[s * tt:(s + 1) * tt, :])
            for j in range(DISPATCH_EP):
                x_ref[j, pl.ds(offs[s, j], w), :] += rows[j * w:(j + 1) * w]

    pl.when(all_short)(functools.partial(gather, DISPATCH_WS))
    pl.when(jnp.logical_not(all_short))(functools.partial(gather, DISPATCH_W))

    @pl.when(i == pl.num_programs(1) - 1)
    def _():
        o_ref[...] = x_ref[:, 0:cap, :].astype(o_ref.dtype)


def _dispatch(starts, hn, p, cap):
    t, d = hn.shape
    tb = DISPATCH_TT * DISPATCH_SUB
    ep = DISPATCH_EP
    body = functools.partial(_dispatch_body, cap=cap, blocks_per_win=DISPATCH_TT // SELECT_BLK)
    return pl.pallas_call(
        body,
        grid_spec=pltpu.PrefetchScalarGridSpec(
            num_scalar_prefetch=1,
            grid=(N_EXPERTS // ep, t // tb),
            in_specs=[pl.BlockSpec((tb, d), lambda e, i, s: (i, 0)),
                      pl.BlockSpec((N_EXPERTS, tb), lambda e, i, s: (0, i))],
            out_specs=pl.BlockSpec((ep, cap, d), lambda e, i, s: (e, 0, 0)),
            scratch_shapes=[pltpu.VMEM((ep, cap + DISPATCH_W, d), _F32)]),
        out_shape=jax.ShapeDtypeStruct((N_EXPERTS, cap, d), _BF16),
        compiler_params=_params("parallel", "arbitrary"),
        name="expert_dispatch",
    )(starts, hn, p)


def _ffn_body(x_ref, wg_ref, wu_ref, wd_ref, o_ref, acc_ref, *, tm):
    fc = pl.program_id(1)

    @pl.when(fc == 0)
    def _():
        acc_ref[...] = jnp.zeros_like(acc_ref)

    wg = wg_ref[0, 0].astype(_BF16)
    wu = wu_ref[0, 0].astype(_BF16)
    wd = wd_ref[0, 0].astype(_BF16)
    for r in range(x_ref.shape[1] // tm):
        rows = slice(r * tm, (r + 1) * tm)
        x = x_ref[0, rows, :]
        a = _mm(x, wg)
        b = _mm(x, wu)
        hid = (a * _sigmoid(a) * b).astype(_BF16)
        acc_ref[rows, :] += _mm(hid, wd)

    @pl.when(fc == pl.num_programs(1) - 1)
    def _():
        o_ref[0] = acc_ref[...].astype(o_ref.dtype)


def _expert_ffn(xe, w_gate, w_up, w_down, layer):
    e, cap, d = xe.shape
    ff = w_gate.shape[3]
    fch = min(FFN_CHUNK, ff)
    tm = min(FFN_TM, cap)
    return pl.pallas_call(
        functools.partial(_ffn_body, tm=tm),
        grid=(e, ff // fch),
        in_specs=[pl.BlockSpec((1, cap, d), lambda e, c: (e, 0, 0)),
                  pl.BlockSpec((1, 1, d, fch), lambda e, c: (layer, e, 0, c)),
                  pl.BlockSpec((1, 1, d, fch), lambda e, c: (layer, e, 0, c)),
                  pl.BlockSpec((1, 1, fch, d), lambda e, c: (layer, e, c, 0))],
        out_specs=pl.BlockSpec((1, cap, d), lambda e, c: (e, 0, 0)),
        out_shape=jax.ShapeDtypeStruct((e, cap, d), _BF16),
        scratch_shapes=[pltpu.VMEM((cap, d), _F32)],
        compiler_params=_params("parallel", "arbitrary"),
        name="expert_ffn",
    )(xe, w_gate, w_up, w_down)


def _combine_body(starts_ref, xmid_ref, pt_ref, afft_ref, g_ref, ye_ref, o_ref, sbuf_ref, bbuf_ref, sem_ref,
                  *, cap, final_norm):
    i = pl.program_id(0)
    n = pl.num_programs(0)
    tt = COMBINE_TT

    def windows(tile, slot):
        per_expert = []
        all_short = None
        for e in range(N_EXPERTS):
            start = starts_ref[tile * N_EXPERTS + e]
            end = starts_ref[(tile + 1) * N_EXPERTS + e]
            aligned = (start // BF16_ROWS) * BF16_ROWS
            pair = []
            for w, buf in ((COMBINE_WS, sbuf_ref), (COMBINE_W, bbuf_ref)):
                a0 = pl.multiple_of(jnp.minimum(aligned, cap - w), BF16_ROWS)
                copy = pltpu.make_async_copy(ye_ref.at[e, pl.ds(a0, w), :], buf.at[slot, pl.ds(e * w, w), :],
                                             sem_ref.at[slot, e])
                pair.append((a0, copy))
            per_expert.append(pair)
            short = end - pair[0][0] <= COMBINE_WS
            all_short = short if all_short is None else jnp.logical_and(all_short, short)
        return all_short, per_expert

    def start_tile(tile, slot):
        all_short, per_expert = windows(tile, slot)
        for k in range(2):
            @pl.when(all_short if k == 0 else jnp.logical_not(all_short))
            def _():
                for pair in per_expert:
                    pair[k][1].start()

    @pl.when(i == 0)
    def _():
        start_tile(0, 0)

    @pl.when(i + 1 < n)
    def _():
        start_tile(i + 1, lax.rem(i + 1, 2))

    slot = lax.rem(i, 2)
    all_short, per_expert = windows(i, slot)

    def expand(k, w, buf):
        group = max(1, LANES // w)
        width = group * w
        lane = lax.broadcasted_iota(jnp.int32, (tt, width), 1)
        his, los = [], []
        for e0 in range(0, N_EXPERTS, group):
            rel = gate = None
            for j in range(group):
                e = e0 + j
                per_expert[e][k][1].wait()
                rel_e = pt_ref[:, e:e + 1] - 1 - per_expert[e][k][0]
                gate_e = afft_ref[:, e:e + 1]
                rel = rel_e if j == 0 else jnp.where(lane >= j * w, rel_e, rel)
                gate = gate_e if j == 0 else jnp.where(lane >= j * w, gate_e, gate)
            piece = jnp.where(rel == (lane & (w - 1)), gate, 0.0)
            hi = piece.astype(_BF16)
            his.append(hi)
            los.append((piece - hi.astype(_F32)).astype(_BF16))
        lhs = jnp.concatenate([jnp.concatenate(his, axis=1), jnp.concatenate(los, axis=1)], axis=0)
        res = _mm(lhs, buf[slot])
        acc = xmid_ref[...] + res[:tt] + res[tt:]
        o_ref[...] = _rms(acc, g_ref[...]) if final_norm else acc

    pl.when(all_short)(functools.partial(expand, 0, COMBINE_WS, sbuf_ref))
    pl.when(jnp.logical_not(all_short))(functools.partial(expand, 1, COMBINE_W, bbuf_ref))


def _combine(starts, xmid, p_t, aff_t, g_final, ye, cap, final_norm):
    t, d = xmid.shape
    tt = COMBINE_TT
    return pl.pallas_call(
        functools.partial(_combine_body, cap=cap, final_norm=final_norm),
        grid_spec=pltpu.PrefetchScalarGridSpec(
            num_scalar_prefetch=1,
            grid=(t // tt,),
            in_specs=[pl.BlockSpec((tt, d), lambda i, s: (i, 0)),
                      pl.BlockSpec((tt, N_EXPERTS), lambda i, s: (i, 0)),
                      pl.BlockSpec((tt, N_EXPERTS), lambda i, s: (i, 0)),
                      pl.BlockSpec((1, d), lambda i, s: (0, 0)),
                      pl.BlockSpec(memory_space=pl.ANY)],
            out_specs=pl.BlockSpec((tt, d), lambda i, s: (i, 0)),
            scratch_shapes=[pltpu.VMEM((2, N_EXPERTS * COMBINE_WS, d), _BF16),
                            pltpu.VMEM((2, N_EXPERTS * COMBINE_W, d), _BF16),
                            pltpu.SemaphoreType.DMA((2, N_EXPERTS))]),
        out_shape=jax.ShapeDtypeStruct((t, d), _F32),
        compiler_params=_params("arbitrary"),
        name="expert_combine",
    )(starts, xmid, p_t, aff_t, g_final, ye)


def _rope_tables(seq):
    half = ATT_HD // 2
    inv_freq = ROPE_THETA ** (-jnp.arange(half, dtype=_F32) / half)
    ang = jnp.arange(seq, dtype=_F32)[:, None] * inv_freq[None, :]
    cos, sin = jnp.cos(ang), jnp.sin(ang)
    return jnp.concatenate([cos, cos], axis=-1), jnp.concatenate([-sin, sin], axis=-1)


def _layer(x, batch, seq, w, final_g, final_norm):
    t, d = x.shape
    q, k, v, z, gates, aqkv, bg = _project(x, w, seq)
    g_t =bg[:, :4 * LA_HEADS].reshape(t // DELTA_CHUNK, DELTA_CHUNK, 4 * LA_HEADS).transpose(0, 2, 1)
    o_f, o_b = _delta_rule(q, k, v, bg, g_t, batch, seq)
    o_att = _window_attention(aqkv, w["sink"], batch, seq)
    xmid, hn, aff_t = _merge(o_f, o_b, z, gates, o_att, x, w["w_a"], w["w_b"], w["w_o"],
                             w["la_norm_g"], w["norm_ffn_g"], w["w_r"])

    cap = max(1, CAPACITY_FACTOR * t // N_EXPERTS)
    assert cap >= COMBINE_W and cap % min(FFN_TM, cap) == 0 and cap % BF16_ROWS == 0
    p, starts = _select(aff_t.T, cap)
    starts = starts.reshape(-1)
    xe = _dispatch(starts, hn, p, cap)
    ye = _expert_ffn(xe, w["w_gate"], w["w_up"], w["w_down"], w["layer"])
    return _combine(starts, xmid, p.T, aff_t, final_g, ye, cap, final_norm)


def _layer_weights(l, seq, norm_mix_g, w_in, conv_w, la_a_log, la_dt_bias, la_norm_g, attn_sink, w_branch_a,
                   w_branch_b, w_out, norm_ffn_g, w_router, w_exp_gate, w_exp_up, w_exp_down):
    wi = w_in[l]
    c0 = CONV_CH
    c1 = c0 + LA_V
    c2 = c1 + 4 * LA_HEADS
    c3 = c2 + ATT_COLS
    pad = LANES - 4 * LA_HEADS
    zeros = jnp.zeros((2 * LA_HEADS,), _F32)
    lane_pad = lambda v: jnp.pad(jnp.concatenate([zeros, v.reshape(-1)]), (0, pad)).reshape(1, LANES)
    return {
        "norm_mix_g": norm_mix_g[l].reshape(1, -1),
        "w_conv": wi[:, :c0].astype(_BF16),
        "w_z": wi[:, c0:c1].astype(_BF16),
        "w_bg": jnp.pad(wi[:, c1:c2], ((0, 0), (0, pad))).astype(_BF16),
        "w_att": wi[:, c2:c3].astype(_BF16),
        "w_gates": wi[:, c3:].astype(_BF16),
        "a_log": lane_pad(la_a_log[l]),
        "dt_bias": lane_pad(la_dt_bias[l]),
        "rope": _rope_tables(seq),
        "conv_w": conv_w[l],
        "sink": attn_sink[l],
        "w_a": w_branch_a[l].astype(_BF16),
        "w_b": w_branch_b[l].astype(_BF16),
        "w_o": w_out[l].astype(_BF16),
        "la_norm_g": la_norm_g[l].reshape(1, -1),
        "norm_ffn_g": norm_ffn_g[l].reshape(1, -1),
        "w_r": w_router[l],
        "layer": l,
        "w_gate": w_exp_gate,
        "w_up": w_exp_up,
        "w_down": w_exp_down,
    }


def kernel(x_prompt, x_sample, norm_mix_g, w_in, conv_w, la_a_log, la_dt_bias, la_norm_g, attn_sink, w_branch_a, w_branch_b, w_out, norm_ffn_g, w_router, w_exp_gate, w_exp_up, w_exp_down, final_norm_g):
    depth = w_in.shape[0]
    layer_args = (norm_mix_g, w_in, conv_w, la_a_log, la_dt_bias, la_norm_g, attn_sink, w_branch_a, w_branch_b,
                  w_out, norm_ffn_g, w_router, w_exp_gate, w_exp_up, w_exp_down)
    final_g = final_norm_g.reshape(1, -1)

    def trunk(x):
        batch, seq, d = x.shape
        h = x.reshape(batch * seq, d)
        for l in range(depth):
            w = _layer_weights(l, seq, *layer_args)
            h = _layer(h, batch, seq, w, final_g, l == depth - 1)
        return h.reshape(batch, seq, d)

    return trunk(x_prompt), trunk(x_sample)
```

```python
import functools

import jax
import jax.numpy as jnp
from jax import lax
from jax.experimental import pallas as pl
from jax.experimental.pallas import tpu as pltpu

D_MODEL = 1024
LA_HEADS = 8
LA_DK = 128
LA_DV = 128
CONV_W = 4
ATT_Q_HEADS = 8
ATT_KV_HEADS = 2
ATT_HD = 128
WINDOW = 128
ROPE_THETA = 10000.0
N_EXPERTS = 16
EXPERT_FF = 2048
CAPACITY_FACTOR = 2
EPS = 1e-6

LA_QK = LA_HEADS * LA_DK
LA_V = LA_HEADS * LA_DV
ATT_Q = ATT_Q_HEADS * ATT_HD
ATT_KV = ATT_KV_HEADS * ATT_HD
CONV_CH = 2 * LA_QK + LA_V
ATT_COLS = ATT_Q + 2 * ATT_KV

LANES = 128
SUBLANES = 8
BF16_ROWS = 16
VMEM_LIMIT = 56 * 1024 * 1024

DELTA_CHUNK = 128
ATT_BLOCK = 128
ATT_QB = 2
ATT_MASKED = -1e30
PROJ_TM = 512
MERGE_TM = 512
DISPATCH_TT = 256
DISPATCH_SUB = 4
DISPATCH_EP = 4
DISPATCH_W = DISPATCH_TT + BF16_ROWS
DISPATCH_WS = 64
COMBINE_TT = 256
COMBINE_W = 384
COMBINE_WS = 64
FFN_TM = 1024
FFN_CHUNK = 512
SELECT_BLK = 128

_BF16 = jnp.bfloat16
_F32 = jnp.float32


def _params(*sem):
    return pltpu.CompilerParams(dimension_semantics=sem, vmem_limit_bytes=VMEM_LIMIT)


def _rms(x, g):
    return x * lax.rsqrt(jnp.mean(x * x, axis=-1, keepdims=True) + EPS) * g


def _sigmoid(x):
    return 1.0 / (1.0 + jnp.exp(-x))


def _mm(a, b):
    return jnp.dot(a, b, preferred_element_type=_F32)


def _mm_nt(a, b):
    return lax.dot_general(a, b, (((1,), (1,)), ((), ())), preferred_element_type=_F32)


def _mm_tn(a, b):
    return lax.dot_general(a, b, (((0,), (0,)), ((), ())), preferred_element_type=_F32)


def _split3(x):
    h = x.astype(_BF16)
    r = x - h.astype(_F32)
    m = r.astype(_BF16)
    l = (r - m.astype(_F32)).astype(_BF16)
    return h, m, l


def _mm_exact_lhs(a01, x):
    h, m, l = _split3(x)
    return _mm(a01, h) + _mm(a01, m) + _mm(a01, l)


def _proj_body(x_ref, xp_ref, xn_ref, g_ref, wc_ref, wz_ref, wg_ref, wa_ref, wb_ref, cw_ref, alog_ref, dt_ref,
               cos_ref, sin_ref, q_ref, k_ref, v_ref, z_ref, gates_ref, aqkv_ref, bg_ref, hn_ref, ext_ref,
               *, tiles_per_seq, tm):
    i = pl.program_id(0)
    pos = lax.rem(i, tiles_per_seq)
    keep_prev = jnp.where(pos == 0, 0.0, 1.0)
    keep_next = jnp.where(pos == tiles_per_seq - 1, 0.0, 1.0)
    hr = BF16_ROWS
    g = g_ref[...]
    hn_ref[0:hr, :] = _rms(xp_ref[...], g).astype(_BF16)
    hn_ref[hr:hr + tm, :] = _rms(x_ref[...], g).astype(_BF16)
    hn_ref[hr + tm:, :] = _rms(xn_ref[...], g).astype(_BF16)
    hn = hn_ref[hr:hr + tm, :]

    for c, o_ref in enumerate((q_ref, k_ref, v_ref)):
        sl = slice(c * LA_QK, (c + 1) * LA_QK)
        raw = _mm(hn_ref[...], wc_ref[:, sl])
        ext_ref[0:hr, :] = raw[0:hr] * keep_prev
        ext_ref[hr:hr + tm, :] = raw[hr:hr + tm]
        ext_ref[hr + tm:, :] = raw[hr + tm:] * keep_next
        w = cw_ref[:, sl]
        y = (ext_ref[hr - 2:hr - 2 + tm, :] * w[0:1]
             + ext_ref[hr - 1:hr - 1 + tm, :] * w[1:2]
             + ext_ref[hr:hr + tm, :] * w[2:3]
             + ext_ref[hr + 1:hr + 1 + tm, :] * w[3:4])
        y = y * _sigmoid(y)
        if c == 2:
            o_ref[...] = y.astype(o_ref.dtype)
            continue
        scale = LA_DK ** -0.5 if c == 0 else 1.0
        for h in range(LA_HEADS):
            hs = slice(h * LA_DK, (h + 1) * LA_DK)
            yh = y[:, hs]
            yh = yh * lax.rsqrt(jnp.sum(yh * yh, axis=-1, keepdims=True) + EPS)
            o_ref[:, hs] = (yh * scale).astype(o_ref.dtype)

    z_ref[...] = _mm(hn, wz_ref[...]).astype(z_ref.dtype)
    for c in range(2):
        sl = slice(c * D_MODEL, (c + 1) * D_MODEL)
        gates_ref[:, sl] = _sigmoid(_mm(hn, wg_ref[:, sl])).astype(gates_ref.dtype)

    acc = _mm(hn, wa_ref[...])
    cos, sin = cos_ref[...], sin_ref[...]
    qcos, qsin = cos * (ATT_HD ** -0.5), sin * (ATT_HD ** -0.5)
    for h in range(ATT_Q_HEADS + ATT_KV_HEADS):
        sl = slice(h * ATT_HD, (h + 1) * ATT_HD)
        xh = acc[:, sl]
        c, s = (qcos, qsin) if h < ATT_Q_HEADS else (cos, sin)
        aqkv_ref[:, sl] = (xh * c + pltpu.roll(xh, ATT_HD // 2, 1) * s).astype(aqkv_ref.dtype)
    aqkv_ref[:, ATT_Q + ATT_KV:] = acc[:, ATT_Q + ATT_KV:].astype(aqkv_ref.dtype)

    acc = _mm(hn, wb_ref[...])
    zz = acc + dt_ref[...]
    sp = jnp.maximum(zz, 0.0) + jnp.log1p(jnp.exp(-jnp.abs(zz)))
    lane = lax.broadcasted_iota(jnp.int32, acc.shape, 1)
    decay = jnp.where(lane < 4 * LA_HEADS, -jnp.exp(alog_ref[...]) * sp, 0.0)
    bg_ref[...] = jnp.where(lane < 2 * LA_HEADS, _sigmoid(acc), decay)


def _project(x, w, seq):
    t, d = x.shape
    tm = min(PROJ_TM, seq)
    halo_blocks = tm // BF16_ROWS
    n_halo = t // BF16_ROWS
    tiles_per_seq = seq // tm
    row = lambda i: (i, 0)
    fix = lambda i: (0, 0)
    resident = lambda shape: pl.BlockSpec(shape, fix, pipeline_mode=pl.Buffered(1))
    rope_spec = pl.BlockSpec((tm, ATT_HD), lambda i: (lax.rem(i, tiles_per_seq), 0))
    bf = lambda n: jax.ShapeDtypeStruct((t, n), _BF16)
    return pl.pallas_call(
        functools.partial(_proj_body, tiles_per_seq=tiles_per_seq, tm=tm),
        grid=(t // tm,),
        in_specs=[pl.BlockSpec((tm, d), row),
                  pl.BlockSpec((BF16_ROWS, d), lambda i: (jnp.maximum(i * halo_blocks - 1, 0), 0)),
                  pl.BlockSpec((BF16_ROWS, d), lambda i: (jnp.minimum((i + 1) * halo_blocks, n_halo - 1), 0)),
                  resident((1, d)),
                  resident((d, CONV_CH)), resident((d, LA_V)), resident((d, 2 * D_MODEL)),
                  resident((d, ATT_COLS)), resident((d, LANES)),
                  resident((CONV_W, CONV_CH)), resident((1, LANES)), resident((1, LANES)),
                  rope_spec, rope_spec],
        out_specs=[pl.BlockSpec((tm, LA_QK), row), pl.BlockSpec((tm, LA_QK), row), pl.BlockSpec((tm, LA_V), row),
                   pl.BlockSpec((tm, LA_V), row), pl.BlockSpec((tm, 2 * D_MODEL), row),
                   pl.BlockSpec((tm, ATT_COLS), row), pl.BlockSpec((tm, LANES), row)],
        out_shape=[bf(LA_QK), bf(LA_QK), bf(LA_V), bf(LA_V), bf(2 * D_MODEL), bf(ATT_COLS),
                   jax.ShapeDtypeStruct((t, LANES), _F32)],
        scratch_shapes=[pltpu.VMEM((tm + 2 * BF16_ROWS, d), _BF16),
                        pltpu.VMEM((tm + 2 * BF16_ROWS, LA_QK), _F32)],
        compiler_params=_params("parallel"),
        name="in_proj",
    )(x, x, x, w["norm_mix_g"], w["w_conv"], w["w_z"], w["w_gates"], w["w_att"], w["w_bg"],
      w["conv_w"], w["a_log"], w["dt_bias"], *w["rope"])


def _delta_body(qf, kf, vf, bgf, gtf, qb, kb, vb, bgb, gtb, of_ref, ob_ref, state_ref):
    @pl.when(pl.program_id(1) == 0)
    def _():
        state_ref[...] = jnp.zeros_like(state_ref)

    c = DELTA_CHUNK
    row = lax.broadcasted_iota(jnp.int32, (c, c), 0)
    col = lax.broadcasted_iota(jnp.int32, (c, c), 1)
    eye = jnp.where(row == col, 1.0, 0.0)
    refs = ((qf, kf, vf, bgf, gtf, of_ref), (qb, kb, vb, bgb, gtb, ob_ref))

    chains = []
    for d, (q_ref, k_ref, v_ref, bg_ref, gt_ref, o_ref) in enumerate(refs):
        incl = (row >= col) if d == 0 else (row <= col)
        strict = (row > col) if d == 0 else (row < col)
        incl_b = jnp.where(incl, 1.0, 0.0).astype(_BF16)
        incl_t = jnp.where((row <= col) if d == 0 else (row >= col), 1.0, 0.0).astype(_BF16)
        bg = bg_ref[...]
        gc_cols = _mm_exact_lhs(incl_b, bg)
        h3 = _split3(gt_ref[0])
        gc_rows = _mm(h3[0], incl_t) + _mm(h3[1], incl_t) + _mm(h3[2], incl_t)
        total_row = c - 1 if d == 0 else 0
        for h in range(LA_HEADS):
            bcol = LA_HEADS * d + h
            gcol = 2 * LA_HEADS + LA_HEADS * d + h
            chains.append(dict(
                d=d, h=h, hs=slice(h * LA_DK, (h + 1) * LA_DK), incl=incl, strict=strict,
                q_ref=q_ref, k_ref=k_ref, v_ref=v_ref, o_ref=o_ref,
                beta=bg[:, bcol:bcol + 1], gc=gc_cols[:, gcol:gcol + 1], gc_r=gc_rows[gcol:gcol + 1, :],
                g_tot=gc_cols[total_row:total_row + 1, gcol:gcol + 1]))

    for ch in chains:
        q = ch["q_ref"][:, ch["hs"]].astype(_F32)
        k = ch["k_ref"][:, ch["hs"]]
        kf32 = k.astype(_F32)
        ch["kb"] = kf32 * ch["beta"]
        ch["egc"] = jnp.exp(ch["gc"])
        ch["a_qk"] = _mm_nt(jnp.concatenate([ch["kb"], q], axis=0).astype(_BF16), k)
        ch["qg"] = (q * ch["egc"]).astype(_BF16)
        ch["k_dec"] = (kf32 * jnp.exp(ch["g_tot"] - ch["gc"])).astype(_BF16)

    for ch in chains:
        incl = ch["incl"]
        decay = jnp.where(incl, jnp.exp(jnp.where(incl, ch["gc"] - ch["gc_r"], 0.0)), 0.0)
        l_mat = jnp.where(ch["strict"], ch["a_qk"][:c] * decay, 0.0)
        ch["qk"] = (ch["a_qk"][c:] * decay).astype(_BF16)
        ch["y"] = eye - l_mat
        l_b = l_mat.astype(_BF16)
        ch["m_b"] = _mm(l_b, l_b).astype(_BF16)
    p = 2
    while p < c:
        last = 2 * p >= c
        for ch in chains:
            lhs = ch["y"].astype(_BF16) if last else jnp.concatenate([ch["m_b"], ch["y"].astype(_BF16)], axis=0)
            prod = _mm(lhs, ch["m_b"])
            if last:
                ch["y"] = ch["y"] + prod
            else:
                ch["m_b"] = prod[:c].astype(_BF16)
                ch["y"] = ch["y"] + prod[c:]
        p *= 2

    for ch in chains:
        v = ch["v_ref"][:, ch["hs"]].astype(_F32)
        rhs = jnp.concatenate([v * ch["beta"], ch["kb"] * ch["egc"]], axis=1).astype(_BF16)
        ch["uw"] = _mm(ch["y"].astype(_BF16), rhs)
    for ch in chains:
        ch["state"] = state_ref[ch["d"], ch["h"]]
        w_b = ch["uw"][:, LA_DV:].astype(_BF16)
        ch["ws_qs"] = _mm(jnp.concatenate([w_b, ch["qg"]], axis=0), ch["state"].astype(_BF16))
    for ch in chains:
        v_new_b = (ch["uw"][:, :LA_DV] - ch["ws_qs"][:c]).astype(_BF16)
        ch["o_ref"][:, ch["hs"]] = ch["ws_qs"][c:] + _mm(ch["qk"], v_new_b)
        state_ref[ch["d"], ch["h"]] = ch["state"] * jnp.exp(ch["g_tot"]) + _mm_tn(ch["k_dec"], v_new_b)


def _delta_rule(q, k, v, bg, g_t, batch, seq):
    t = q.shape[0]
    c = DELTA_CHUNK
    n = seq // c
    fwd = lambda b, j: (b * n + j, 0)
    bwd = lambda b, j: (b * n + n - 1 - j, 0)
    fwd3 = lambda b, j: (b * n + j, 0, 0)
    bwd3 = lambda b, j: (b * n + n - 1 - j, 0, 0)

    def specs(im, im3):
        return [pl.BlockSpec((c, LA_QK), im), pl.BlockSpec((c, LA_QK), im), pl.BlockSpec((c, LA_V), im),
                pl.BlockSpec((c, LANES), im), pl.BlockSpec((1, 4 * LA_HEADS, c), im3)]

    out = jax.ShapeDtypeStruct((t, LA_V), _F32)
    return pl.pallas_call(
        _delta_body,
        grid=(batch, n),
        in_specs=specs(fwd, fwd3) + specs(bwd, bwd3),
        out_specs=[pl.BlockSpec((c, LA_V), fwd), pl.BlockSpec((c, LA_V), bwd)],
        out_shape=[out, out],
        scratch_shapes=[pltpu.VMEM((2, LA_HEADS, LA_DK, LA_DV), _F32)],
        compiler_params=_params("parallel", "arbitrary"),
        name="delta_rule",
    )(q, k, v, bg, g_t, q, k, v, bg, g_t)


def _attn_body(sink_ref, q_ref, kp_ref, kc_ref, kn_ref, vp_ref, vc_ref, vn_ref, o_ref, *, seq):
    n = pl.program_id(1)
    blk = ATT_BLOCK
    group = ATT_Q_HEADS // ATT_KV_HEADS
    kw = jnp.concatenate([kp_ref[...], kc_ref[...], kn_ref[...]], axis=0)
    vw = jnp.concatenate([vp_ref[...], vc_ref[...], vn_ref[...]], axis=0)
    row = lax.broadcasted_iota(jnp.int32, (blk, 3 * blk), 0)
    col = lax.broadcasted_iota(jnp.int32, (blk, 3 * blk), 1)
    band = jnp.abs(row + blk - col) <= WINDOW

    chains = []
    for qb in range(ATT_QB):
        kpos = (ATT_QB * n + qb - 1) * blk + col
        bias = jnp.where(band & (kpos >= 0) & (kpos < seq), 0.0, ATT_MASKED)
        bias = jnp.concatenate([bias] * group, axis=0)
        for g in range(ATT_KV_HEADS):
            heads = [g * group + j for j in range(group)]
            chains.append(dict(qb=qb, g=g, heads=heads, bias=bias, rows=slice(qb * blk, (qb + 1) * blk),
                               keys=slice(qb * blk, (qb + 3) * blk), cols=slice(g * ATT_HD, (g + 1) * ATT_HD)))
    for ch in chains:
        qs = jnp.concatenate([q_ref[ch["rows"], h * ATT_HD:(h + 1) * ATT_HD] for h in ch["heads"]], axis=0)
        ch["s"] = _mm_nt(qs, kw[ch["keys"], ch["cols"]]) + ch["bias"]
    for ch in chains:
        s = ch["s"]
        sk = jnp.concatenate([jnp.full((blk, 1), sink_ref[h], _F32) for h in ch["heads"]], axis=0)
        m = jnp.maximum(jnp.max(s, axis=-1, keepdims=True), sk)
        p = jnp.exp(s - m)
        ch["denom"] = jnp.sum(p, axis=-1, keepdims=True) + jnp.exp(sk - m)
        ch["p"] = p.astype(_BF16)
    for ch in chains:
        o = _mm(ch["p"], vw[ch["keys"], ch["cols"]]) / ch["denom"]
        for j, h in enumerate(ch["heads"]):
            o_ref[ch["rows"], h * ATT_HD:(h + 1) * ATT_HD] = o[j * blk:(j + 1) * blk].astype(o_ref.dtype)


def _window_attention(aqkv, sink, batch, seq):
    t = aqkv.shape[0]
    blk = ATT_BLOCK
    nb = seq // blk
    ns = nb // ATT_QB
    kcol = ATT_Q // ATT_KV
    prev = lambda b, n: b * nb + jnp.maximum(ATT_QB * n - 1, 0)
    nxt = lambda b, n: b * nb + jnp.minimum(ATT_QB * n + ATT_QB, nb - 1)
    cur = lambda b, n: b * ns + n
    kv_specs = []
    for c in (kcol, kcol + 1):
        kv_specs += [pl.BlockSpec((blk, ATT_KV), functools.partial(lambda b, n, c: (prev(b, n), c), c=c)),
                     pl.BlockSpec((ATT_QB * blk, ATT_KV), functools.partial(lambda b, n, c: (cur(b, n), c), c=c)),
                     pl.BlockSpec((blk, ATT_KV), functools.partial(lambda b, n, c: (nxt(b, n), c), c=c))]
    return pl.pallas_call(
        functools.partial(_attn_body, seq=seq),
        grid=(batch, ns),
        in_specs=[pl.BlockSpec(memory_space=pltpu.SMEM),
                  pl.BlockSpec((ATT_QB * blk, ATT_Q), lambda b, n: (cur(b, n), 0)),
                  *kv_specs],
        out_specs=pl.BlockSpec((ATT_QB * blk, ATT_Q), lambda b, n: (cur(b, n), 0)),
        out_shape=jax.ShapeDtypeStruct((t, ATT_Q), _BF16),
        compiler_params=_params("parallel", "parallel"),
        name="window_attention",
    )(sink, *([aqkv] * 7))


def _merge_body(of_ref, ob_ref, z_ref, ga_ref, gb_ref, oatt_ref, x_ref, wa_ref, wb_ref, wo_ref,
                ng_ref, fg_ref, wr_ref, xmid_ref, hn_ref, aff_ref, oa_ref):
    o = of_ref[...] + ob_ref[...]
    z = z_ref[...].astype(_F32)
    gate = z * _sigmoid(z)
    for h in range(LA_HEADS):
        hs = slice(h * LA_DV, (h + 1) * LA_DV)
        oh = o[:, hs]
        oh = _rms(oh, ng_ref[...])
        oa_ref[:, hs] = (oh * gate[:, hs]).astype(_BF16)
    merged = (ga_ref[...].astype(_F32) * _mm(oa_ref[...], wa_ref[...])
              + gb_ref[...].astype(_F32) * _mm(oatt_ref[...], wb_ref[...]))
    xmid = x_ref[...] + _mm(merged.astype(_BF16), wo_ref[...])
    xmid_ref[...] = xmid
    hn = _rms(xmid, fg_ref[...])
    hn_ref[...] = hn.astype(_BF16)
    tm = hn.shape[0]
    wh, wl, _ = _split3(wr_ref[...])
    hh, hl, _ = _split3(hn)
    r1 = _mm(jnp.concatenate([hh, hl], axis=0), wh)
    logits = r1[:tm] + r1[tm:] + _mm(hh, wl)
    e = jnp.exp(logits - jnp.max(logits, axis=-1, keepdims=True))
    aff_ref[...] = e / jnp.sum(e, axis=-1, keepdims=True)


def _merge(o_f, o_b, z, gates, o_att, x, w_a, w_b, w_o, norm_g, ffn_g, w_r):
    t, d = x.shape
    tm = min(MERGE_TM, t)
    row = lambda i: (i, 0)
    fix = lambda i: (0, 0)
    act = pl.BlockSpec((tm, d), row)
    wspec = pl.BlockSpec((d, d), fix)
    return pl.pallas_call(
        _merge_body,
        grid=(t // tm,),
        in_specs=[act, act, act, act, pl.BlockSpec((tm, d), lambda i: (i, 1)), act, act,
                  wspec, wspec, wspec,
                  pl.BlockSpec((1, LA_DV), fix), pl.BlockSpec((1, d), fix),
                  pl.BlockSpec((d, N_EXPERTS), fix)],
        out_specs=[act, act, pl.BlockSpec((tm, N_EXPERTS), row)],
        out_shape=[jax.ShapeDtypeStruct((t, d), _F32), jax.ShapeDtypeStruct((t, d), _BF16),
                   jax.ShapeDtypeStruct((t, N_EXPERTS), _F32)],
        scratch_shapes=[pltpu.VMEM((tm, d), _BF16)],
        compiler_params=_params("parallel"),
        name="merge_router",
    )(o_f, o_b, z, gates, gates, o_att, x, w_a, w_b, w_o, norm_g, ffn_g, w_r)


def _select_body(aff_ref, p_ref, starts_ref, *, cap, n_blk):
    e = N_EXPERTS
    blk = SELECT_BLK

    def bits_of(x):
        return pltpu.bitcast(x, jnp.int32)

    def bisect(it, prefix):
        cand = prefix | jnp.left_shift(jnp.int32(1), 30 - it)
        cnt = jnp.sum(jnp.where(bits_of(aff_ref[...]) >= cand, 1, 0), axis=1, keepdims=True)
        return jnp.where(cnt >= cap, cand, prefix)

    thr = lax.fori_loop(0, 31, bisect, jnp.zeros((e, 1), jnp.int32))
    n_gt = jnp.sum(jnp.where(bits_of(aff_ref[...]) > thr, 1, 0), axis=1, keepdims=True)
    need = (cap - n_gt).astype(_F32)

    r = lax.broadcasted_iota(jnp.int32, (blk, blk), 0)
    c = lax.broadcasted_iota(jnp.int32, (blk, blk), 1)
    upper = jnp.where(r <= c, 1.0, 0.0).astype(_BF16)
    ones = jnp.ones((SUBLANES, blk), _BF16)

    def block(j, carry):
        eq_c, sel_c, sel_row = carry
        off = pl.multiple_of(j * blk, blk)
        b = bits_of(aff_ref[:, pl.ds(off, blk)])
        gt = b > thr
        eq = b == thr
        eq_b = jnp.where(eq, 1.0, 0.0).astype(_BF16)
        eq_incl = _mm(eq_b, upper) + eq_c
        sel = gt | (eq & (eq_incl <= need))
        sel_b = jnp.where(sel, 1.0, 0.0).astype(_BF16)
        sel_incl = _mm(sel_b, upper) + sel_c
        p_ref[:, pl.ds(off, blk)] = jnp.where(sel, sel_incl, 0.0).astype(jnp.int32)
        starts_ref[pl.ds(j, 1), :] = sel_row[0:1].astype(jnp.int32)
        return (eq_c + jnp.sum(eq_b.astype(_F32), axis=1, keepdims=True),
                sel_c + jnp.sum(sel_b.astype(_F32), axis=1, keepdims=True),
                sel_row + _mm_nt(ones, sel_b))

    zero = jnp.zeros((e, 1), _F32)
    _, _, totals = lax.fori_loop(0, n_blk, block, (zero, zero, jnp.zeros((SUBLANES, e), _F32)))
    starts_ref[n_blk:n_blk + 1, :] = totals[0:1].astype(jnp.int32)


def _select(aff, cap):
    e, t = aff.shape
    n_blk = t // SELECT_BLK
    return pl.pallas_call(
        functools.partial(_select_body, cap=cap, n_blk=n_blk),
        out_shape=[jax.ShapeDtypeStruct((e, t), jnp.int32), jax.ShapeDtypeStruct((n_blk + 1, e), jnp.int32)],
        compiler_params=pltpu.CompilerParams(vmem_limit_bytes=VMEM_LIMIT),
        name="expert_select",
    )(aff)


def _dispatch_body(starts_ref, hn_ref, p_ref, o_ref, *, cap, blocks_per_win):
    eb = pl.program_id(0)
    i = pl.program_id(1)
    tt = DISPATCH_TT
    stride = blocks_per_win * N_EXPERTS
    widths = (min(DISPATCH_WS, cap), min(DISPATCH_W, cap))

    @pl.when(i == 0)
    def _():
        o_ref[...] = jnp.zeros_like(o_ref)

    offs = {}
    all_short = None
    for s in range(DISPATCH_SUB):
        win = i * DISPATCH_SUB + s
        for j in range(DISPATCH_EP):
            e = eb * DISPATCH_EP + j
            start = starts_ref[win * stride + e]
            end = starts_ref[(win + 1) * stride + e]
            aligned = (start // BF16_ROWS) * BF16_ROWS
            offs[s, j] = [pl.multiple_of(jnp.minimum(aligned, cap - w), BF16_ROWS) for w in widths]
            short = end - offs[s, j][0] <= widths[0]
            all_short = short if all_short is None else jnp.logical_and(all_short, short)

    def gather(k):
        w = widths[k]
        slot = lax.broadcasted_iota(jnp.int32, (w, tt), 0)
        for s in range(DISPATCH_SUB):
            onehots = []
            for j in range(DISPATCH_EP):
                e = eb * DISPATCH_EP + j
                rel = p_ref[pl.ds(e, 1), s * tt:(s + 1) * tt] - 1 - offs[s, j][k]
                onehots.append(jnp.where(rel == slot, 1.0, 0.0).astype(_BF16))
            rows = _mm(jnp.concatenate(onehots, axis=0), hn_ref[s * tt:(s + 1) * tt, :]).astype(o_ref.dtype)
            for j in range(DISPATCH_EP):
                dst = (j, pl.ds(offs[s, j][k], w), slice(None))
                o_ref[dst] = o_ref[dst] + rows[j * w:(j + 1) * w]

    pl.when(all_short)(functools.partial(gather, 0))
    pl.when(jnp.logical_not(all_short))(functools.partial(gather, 1))


def _dispatch(starts, hn, p, cap):
    t, d = hn.shape
    tb = DISPATCH_TT * DISPATCH_SUB
    ep = DISPATCH_EP
    body = functools.partial(_dispatch_body, cap=cap, blocks_per_win=DISPATCH_TT // SELECT_BLK)
    return pl.pallas_call(
        body,
        grid_spec=pltpu.PrefetchScalarGridSpec(
            num_scalar_prefetch=1,
            grid=(N_EXPERTS // ep, t // tb),
            in_specs=[pl.BlockSpec((tb, d), lambda e, i, s: (i, 0)),
                      pl.BlockSpec((N_EXPERTS, tb), lambda e, i, s: (0, i))],
            out_specs=pl.BlockSpec((ep, cap, d), lambda e, i, s: (e, 0, 0))),
        out_shape=jax.ShapeDtypeStruct((N_EXPERTS, cap, d), _BF16),
        compiler_params=_params("parallel", "arbitrary"),
        name="expert_dispatch",
    )(starts, hn, p)


def _ffn_body(x_ref, wg_ref, wu_ref, wd_ref, o_ref, acc_ref, *, tm):
    fc = pl.program_id(1)

    @pl.when(fc == 0)
    def _():
        acc_ref[...] = jnp.zeros_like(acc_ref)

    wg = wg_ref[0, 0].astype(_BF16)
    wu = wu_ref[0, 0].astype(_BF16)
    wd = wd_ref[0, 0].astype(_BF16)
    for r in range(x_ref.shape[1] // tm):
        rows = slice(r * tm, (r + 1) * tm)
        x = x_ref[0, rows, :]
        a = _mm(x, wg)
        b = _mm(x, wu)
        hid = (a * _sigmoid(a) * b).astype(_BF16)
        acc_ref[rows, :] += _mm(hid, wd)

    @pl.when(fc == pl.num_programs(1) - 1)
    def _():
        o_ref[0] = acc_ref[...].astype(o_ref.dtype)


def _expert_ffn(xe, w_gate, w_up, w_down, layer):
    e, cap, d = xe.shape
    ff = w_gate.shape[3]
    fch = min(FFN_CHUNK, ff)
    tm = min(FFN_TM, cap)
    return pl.pallas_call(
        functools.partial(_ffn_body, tm=tm),
        grid=(e, ff // fch),
        in_specs=[pl.BlockSpec((1, cap, d), lambda e, c: (e, 0, 0)),
                  pl.BlockSpec((1, 1, d, fch), lambda e, c: (layer, e, 0, c)),
                  pl.BlockSpec((1, 1, d, fch), lambda e, c: (layer, e, 0, c)),
                  pl.BlockSpec((1, 1, fch, d), lambda e, c: (layer, e, c, 0))],
        out_specs=pl.BlockSpec((1, cap, d), lambda e, c: (e, 0, 0)),
        out_shape=jax.ShapeDtypeStruct((e, cap, d), _BF16),
        scratch_shapes=[pltpu.VMEM((cap, d), _F32)],
        compiler_params=_params("parallel", "arbitrary"),
        name="expert_ffn",
    )(xe, w_gate, w_up, w_down)


def _combine_body(starts_ref, xmid_ref, pt_ref, afft_ref, g_ref, ye_ref, o_ref, sbuf_ref, bbuf_ref, sem_ref,
                  *, cap, final_norm, ws, wl):
    i = pl.program_id(0)
    n = pl.num_programs(0)
    tt = COMBINE_TT
    stride = (tt // SELECT_BLK) * N_EXPERTS

    def windows(tile, slot):
        per_expert = []
        all_short = None
        for e in range(N_EXPERTS):
            start = starts_ref[tile * stride + e]
            end = starts_ref[(tile + 1) * stride + e]
            aligned = (start // BF16_ROWS) * BF16_ROWS
            pair = []
            for w, buf in ((ws, sbuf_ref), (wl, bbuf_ref)):
                a0 = pl.multiple_of(jnp.minimum(aligned, cap - w), BF16_ROWS)
                copy = pltpu.make_async_copy(ye_ref.at[e, pl.ds(a0, w), :], buf.at[slot, pl.ds(e * w, w), :],
                                             sem_ref.at[slot, e])
                pair.append((a0, copy))
            per_expert.append(pair)
            short = end - pair[0][0] <= ws
            all_short = short if all_short is None else jnp.logical_and(all_short, short)
        return all_short, per_expert

    def start_tile(tile, slot):
        all_short, per_expert = windows(tile, slot)
        for k in range(2):
            @pl.when(all_short if k == 0 else jnp.logical_not(all_short))
            def _():
                for pair in per_expert:
                    pair[k][1].start()

    @pl.when(i == 0)
    def _():
        start_tile(0, 0)

    @pl.when(i + 1 < n)
    def _():
        start_tile(i + 1, lax.rem(i + 1, 2))

    slot = lax.rem(i, 2)
    all_short, per_expert = windows(i, slot)

    def expand(k, w, buf):
        group = max(1, LANES // w)
        width = group * w
        lane = lax.broadcasted_iota(jnp.int32, (tt, width), 1)
        his, los = [], []
        for e0 in range(0, N_EXPERTS, group):
            rel = gate = None
            for j in range(group):
                e = e0 + j
                per_expert[e][k][1].wait()
                rel_e = pt_ref[:, e:e + 1] - 1 - per_expert[e][k][0]
                gate_e = afft_ref[:, e:e + 1]
                rel = rel_e if j == 0 else jnp.where(lane >= j * w, rel_e, rel)
                gate = gate_e if j == 0 else jnp.where(lane >= j * w, gate_e, gate)
            slot_of_lane = lane if group == 1 else lane & (w - 1)
            piece = jnp.where(rel == slot_of_lane, gate, 0.0)
            hi = piece.astype(_BF16)
            his.append(hi)
            los.append((piece - hi.astype(_F32)).astype(_BF16))
        lhs = jnp.concatenate([jnp.concatenate(his, axis=1), jnp.concatenate(los, axis=1)], axis=0)
        res = _mm(lhs, buf[slot])
        acc = xmid_ref[...] + res[:tt] + res[tt:]
        o_ref[...] = _rms(acc, g_ref[...]) if final_norm else acc

    pl.when(all_short)(functools.partial(expand, 0, ws, sbuf_ref))
    pl.when(jnp.logical_not(all_short))(functools.partial(expand, 1, wl, bbuf_ref))


def _combine(starts, xmid, p_t, aff_t, g_final, ye, cap, final_norm):
    t, d = xmid.shape
    tt = COMBINE_TT
    ws, wl = min(COMBINE_WS, cap), min(COMBINE_W, cap)
    return pl.pallas_call(
        functools.partial(_combine_body, cap=cap, final_norm=final_norm, ws=ws, wl=wl),
        grid_spec=pltpu.PrefetchScalarGridSpec(
            num_scalar_prefetch=1,
            grid=(t // tt,),
            in_specs=[pl.BlockSpec((tt, d), lambda i, s: (i, 0)),
                      pl.BlockSpec((tt, N_EXPERTS), lambda i, s: (i, 0)),
                      pl.BlockSpec((tt, N_EXPERTS), lambda i, s: (i, 0)),
                      pl.BlockSpec((1, d), lambda i, s: (0, 0)),
                      pl.BlockSpec(memory_space=pl.ANY)],
            out_specs=pl.BlockSpec((tt, d), lambda i, s: (i, 0)),
            scratch_shapes=[pltpu.VMEM((2, N_EXPERTS * ws, d), _BF16),
                            pltpu.VMEM((2, N_EXPERTS * wl, d), _BF16),
                            pltpu.SemaphoreType.DMA((2, N_EXPERTS))]),
        out_shape=jax.ShapeDtypeStruct((t, d), _F32),
        compiler_params=_params("arbitrary"),
        name="expert_combine",
    )(starts, xmid, p_t, aff_t, g_final, ye)


def _rope_tables(seq):
    half = ATT_HD // 2
    inv_freq = ROPE_THETA ** (-jnp.arange(half, dtype=_F32) / half)
    ang = jnp.arange(seq, dtype=_F32)[:, None] * inv_freq[None, :]
    cos, sin = jnp.cos(ang), jnp.sin(ang)
    return jnp.concatenate([cos, cos], axis=-1), jnp.concatenate([-sin, sin], axis=-1)


def _layer(x, batch, seq, w, final_g, final_norm):
    t, d = x.shape
    q, k, v, z, gates, aqkv, bg = _project(x, w, seq)
    g_t =bg[:, :4 * LA_HEADS].reshape(t // DELTA_CHUNK, DELTA_CHUNK, 4 * LA_HEADS).transpose(0, 2, 1)
    o_f, o_b = _delta_rule(q, k, v, bg, g_t, batch, seq)
    o_att = _window_attention(aqkv, w["sink"], batch, seq)
    xmid, hn, aff_t = _merge(o_f, o_b, z, gates, o_att, x, w["w_a"], w["w_b"], w["w_o"],
                             w["la_norm_g"], w["norm_ffn_g"], w["w_r"])

    cap = max(1, CAPACITY_FACTOR * t // N_EXPERTS)
    assert cap % LANES == 0 and cap % min(FFN_TM, cap) == 0
    p, starts = _select(aff_t.T, cap)
    starts = starts.reshape(-1)
    xe = _dispatch(starts, hn, p, cap)
    ye = _expert_ffn(xe, w["w_gate"], w["w_up"], w["w_down"], w["layer"])
    return _combine(starts, xmid, p.T, aff_t, final_g, ye, cap, final_norm)


def _layer_weights(l, seq, norm_mix_g, w_in, conv_w, la_a_log, la_dt_bias, la_norm_g, attn_sink, w_branch_a,
                   w_branch_b, w_out, norm_ffn_g, w_router, w_exp_gate, w_exp_up, w_exp_down):
    wi = w_in[l]
    c0 = CONV_CH
    c1 = c0 + LA_V
    c2 = c1 + 4 * LA_HEADS
    c3 = c2 + ATT_COLS
    pad = LANES - 4 * LA_HEADS
    zeros = jnp.zeros((2 * LA_HEADS,), _F32)
    lane_pad = lambda v: jnp.pad(jnp.concatenate([zeros, v.reshape(-1)]), (0, pad)).reshape(1, LANES)
    return {
        "norm_mix_g": norm_mix_g[l].reshape(1, -1),
        "w_conv": wi[:, :c0].astype(_BF16),
        "w_z": wi[:, c0:c1].astype(_BF16),
        "w_bg": jnp.pad(wi[:, c1:c2], ((0, 0), (0, pad))).astype(_BF16),
        "w_att": wi[:, c2:c3].astype(_BF16),
        "w_gates": wi[:, c3:].astype(_BF16),
        "a_log": lane_pad(la_a_log[l]),
        "dt_bias": lane_pad(la_dt_bias[l]),
        "rope": _rope_tables(seq),
        "conv_w": conv_w[l],
        "sink": attn_sink[l],
        "w_a": w_branch_a[l].astype(_BF16),
        "w_b": w_branch_b[l].astype(_BF16),
        "w_o": w_out[l].astype(_BF16),
        "la_norm_g": la_norm_g[l].reshape(1, -1),
        "norm_ffn_g": norm_ffn_g[l].reshape(1, -1),
        "w_r": w_router[l],
        "layer": l,
        "w_gate": w_exp_gate,
        "w_up": w_exp_up,
        "w_down": w_exp_down,
    }


def kernel(x_prompt, x_sample, norm_mix_g, w_in, conv_w, la_a_log, la_dt_bias, la_norm_g, attn_sink, w_branch_a, w_branch_b, w_out, norm_ffn_g, w_router, w_exp_gate, w_exp_up, w_exp_down, final_norm_g):
    depth = w_in.shape[0]
    layer_args = (norm_mix_g, w_in, conv_w, la_a_log, la_dt_bias, la_norm_g, attn_sink, w_branch_a, w_branch_b,
                  w_out, norm_ffn_g, w_router, w_exp_gate, w_exp_up, w_exp_down)
    final_g = final_norm_g.reshape(1, -1)

    def trunk(x):
        batch, seq, d = x.shape
        h = x.reshape(batch * seq, d)
        for l in range(depth):
            w = _layer_weights(l, seq, *layer_args)
            h = _layer(h, batch, seq, w, final_g, l == depth - 1)
        return h.reshape(batch, seq, d)

    return trunk(x_prompt), trunk(x_sample)
```

```python
import functools

import jax
import jax.numpy as jnp
from jax import lax
from jax.experimental import pallas as pl
from jax.experimental.pallas import tpu as pltpu

D_MODEL = 1024
LA_HEADS = 8
LA_DK = 128
LA_DV = 128
CONV_W = 4
ATT_Q_HEADS = 8
ATT_KV_HEADS = 2
ATT_HD = 128
WINDOW = 128
ROPE_THETA = 10000.0
N_EXPERTS = 16
EXPERT_FF = 2048
CAPACITY_FACTOR = 2
EPS = 1e-6

LA_QK = LA_HEADS * LA_DK
LA_V = LA_HEADS * LA_DV
ATT_Q = ATT_Q_HEADS * ATT_HD
ATT_KV = ATT_KV_HEADS * ATT_HD
CONV_CH = 2 * LA_QK + LA_V
ATT_COLS = ATT_Q + 2 * ATT_KV

LANES = 128
SUBLANES = 8
BF16_ROWS = 16
VMEM_LIMIT = 56 * 1024 * 1024

DELTA_CHUNK = 128
ATT_BLOCK = 128
ATT_QB = 8
ATT_MASKED = -1e30
PROJ_TM = 512
MERGE_TM = 512
DISPATCH_TT = 256
DISPATCH_SUB = 4
DISPATCH_EP = 4
DISPATCH_W = DISPATCH_TT + BF16_ROWS
DISPATCH_WS = 64
COMBINE_TT = 256
COMBINE_W = 384
COMBINE_WS = 64
FFN_TM = 1024
FFN_CHUNK = 512
SELECT_BLK = 128

_BF16 = jnp.bfloat16
_F32 = jnp.float32


def _params(*sem):
    return pltpu.CompilerParams(dimension_semantics=sem, vmem_limit_bytes=VMEM_LIMIT)


def _rms(x, g):
    return x * lax.rsqrt(jnp.mean(x * x, axis=-1, keepdims=True) + EPS) * g


def _sigmoid(x):
    return 1.0 / (1.0 + jnp.exp(-x))


def _mm(a, b):
    return jnp.dot(a, b, preferred_element_type=_F32)


def _mm_nt(a, b):
    return lax.dot_general(a, b, (((1,), (1,)), ((), ())), preferred_element_type=_F32)


def _mm_tn(a, b):
    return lax.dot_general(a, b, (((0,), (0,)), ((), ())), preferred_element_type=_F32)


def _split3(x):
    h = x.astype(_BF16)
    r = x - h.astype(_F32)
    m = r.astype(_BF16)
    l = (r - m.astype(_F32)).astype(_BF16)
    return h, m, l


def _mm_exact_lhs(a01, x):
    h, m, l = _split3(x)
    return _mm(a01, h) + _mm(a01, m) + _mm(a01, l)


def _proj_body(x_ref, xp_ref, xn_ref, g_ref, wc_ref, wz_ref, wg_ref, wa_ref, wb_ref, cw_ref, alog_ref, dt_ref,
               cos_ref, sin_ref, q_ref, k_ref, v_ref, z_ref, gates_ref, aqkv_ref, bg_ref, hn_ref, ext_ref,
               *, tiles_per_seq, tm):
    i = pl.program_id(0)
    pos = lax.rem(i, tiles_per_seq)
    keep_prev = jnp.where(pos == 0, 0.0, 1.0)
    keep_next = jnp.where(pos == tiles_per_seq - 1, 0.0, 1.0)
    hr = BF16_ROWS
    g = g_ref[...]
    hn_ref[0:hr, :] = _rms(xp_ref[...], g).astype(_BF16)
    hn_ref[hr:hr + tm, :] = _rms(x_ref[...], g).astype(_BF16)
    hn_ref[hr + tm:, :] = _rms(xn_ref[...], g).astype(_BF16)
    hn = hn_ref[hr:hr + tm, :]

    for c, o_ref in enumerate((q_ref, k_ref, v_ref)):
        sl = slice(c * LA_QK, (c + 1) * LA_QK)
        raw = _mm(hn_ref[...], wc_ref[:, sl])
        ext_ref[0:hr, :] = raw[0:hr] * keep_prev
        ext_ref[hr:hr + tm, :] = raw[hr:hr + tm]
        ext_ref[hr + tm:, :] = raw[hr + tm:] * keep_next
        w = cw_ref[:, sl]
        y = (ext_ref[hr - 2:hr - 2 + tm, :] * w[0:1]
             + ext_ref[hr - 1:hr - 1 + tm, :] * w[1:2]
             + ext_ref[hr:hr + tm, :] * w[2:3]
             + ext_ref[hr + 1:hr + 1 + tm, :] * w[3:4])
        y = y * _sigmoid(y)
        if c == 2:
            o_ref[...] = y.astype(o_ref.dtype)
            continue
        scale = LA_DK ** -0.5 if c == 0 else 1.0
        for h in range(LA_HEADS):
            hs = slice(h * LA_DK, (h + 1) * LA_DK)
            yh = y[:, hs]
            yh = yh * lax.rsqrt(jnp.sum(yh * yh, axis=-1, keepdims=True) + EPS)
            o_ref[:, hs] = (yh * scale).astype(o_ref.dtype)

    z_ref[...] = _mm(hn, wz_ref[...]).astype(z_ref.dtype)
    for c in range(2):
        sl = slice(c * D_MODEL, (c + 1) * D_MODEL)
        gates_ref[:, sl] = _sigmoid(_mm(hn, wg_ref[:, sl])).astype(gates_ref.dtype)

    acc = _mm(hn, wa_ref[...])
    cos, sin = cos_ref[...], sin_ref[...]
    qcos, qsin = cos * (ATT_HD ** -0.5), sin * (ATT_HD ** -0.5)
    for h in range(ATT_Q_HEADS + ATT_KV_HEADS):
        sl = slice(h * ATT_HD, (h + 1) * ATT_HD)
        xh = acc[:, sl]
        c, s = (qcos, qsin) if h < ATT_Q_HEADS else (cos, sin)
        aqkv_ref[:, sl] = (xh * c + pltpu.roll(xh, ATT_HD // 2, 1) * s).astype(aqkv_ref.dtype)
    aqkv_ref[:, ATT_Q + ATT_KV:] = acc[:, ATT_Q + ATT_KV:].astype(aqkv_ref.dtype)

    acc = _mm(hn, wb_ref[...])
    zz = acc + dt_ref[...]
    sp = jnp.maximum(zz, 0.0) + jnp.log1p(jnp.exp(-jnp.abs(zz)))
    lane = lax.broadcasted_iota(jnp.int32, acc.shape, 1)
    decay = jnp.where(lane < 4 * LA_HEADS, -jnp.exp(alog_ref[...]) * sp, 0.0)
    bg_ref[...] = jnp.where(lane < 2 * LA_HEADS, _sigmoid(acc), decay)


def _project(x, w, seq):
    t, d = x.shape
    tm = min(PROJ_TM, seq)
    halo_blocks = tm // BF16_ROWS
    n_halo = t // BF16_ROWS
    tiles_per_seq = seq // tm
    row = lambda i: (i, 0)
    fix = lambda i: (0, 0)
    resident = lambda shape: pl.BlockSpec(shape, fix, pipeline_mode=pl.Buffered(1))
    rope_spec = pl.BlockSpec((tm, ATT_HD), lambda i: (lax.rem(i, tiles_per_seq), 0))
    bf = lambda n: jax.ShapeDtypeStruct((t, n), _BF16)
    return pl.pallas_call(
        functools.partial(_proj_body, tiles_per_seq=tiles_per_seq, tm=tm),
        grid=(t // tm,),
        in_specs=[pl.BlockSpec((tm, d), row),
                  pl.BlockSpec((BF16_ROWS, d), lambda i: (jnp.maximum(i * halo_blocks - 1, 0), 0)),
                  pl.BlockSpec((BF16_ROWS, d), lambda i: (jnp.minimum((i + 1) * halo_blocks, n_halo - 1), 0)),
                  resident((1, d)),
                  resident((d, CONV_CH)), resident((d, LA_V)), resident((d, 2 * D_MODEL)),
                  resident((d, ATT_COLS)), resident((d, LANES)),
                  resident((CONV_W, CONV_CH)), resident((1, LANES)), resident((1, LANES)),
                  rope_spec, rope_spec],
        out_specs=[pl.BlockSpec((tm, LA_QK), row), pl.BlockSpec((tm, LA_QK), row), pl.BlockSpec((tm, LA_V), row),
                   pl.BlockSpec((tm, LA_V), row), pl.BlockSpec((tm, 2 * D_MODEL), row),
                   pl.BlockSpec((tm, ATT_COLS), row), pl.BlockSpec((tm, LANES), row)],
        out_shape=[bf(LA_QK), bf(LA_QK), bf(LA_V), bf(LA_V), bf(2 * D_MODEL), bf(ATT_COLS),
                   jax.ShapeDtypeStruct((t, LANES), _F32)],
        scratch_shapes=[pltpu.VMEM((tm + 2 * BF16_ROWS, d), _BF16),
                        pltpu.VMEM((tm + 2 * BF16_ROWS, LA_QK), _F32)],
        compiler_params=_params("parallel"),
        name="in_proj",
    )(x, x, x, w["norm_mix_g"], w["w_conv"], w["w_z"], w["w_gates"], w["w_att"], w["w_bg"],
      w["conv_w"], w["a_log"], w["dt_bias"], *w["rope"])


def _delta_body(qf, kf, vf, bgf, gtf, qb, kb, vb, bgb, gtb, of_ref, ob_ref, state_ref):
    @pl.when(pl.program_id(1) == 0)
    def _():
        state_ref[...] = jnp.zeros_like(state_ref)

    c = DELTA_CHUNK
    row = lax.broadcasted_iota(jnp.int32, (c, c), 0)
    col = lax.broadcasted_iota(jnp.int32, (c, c), 1)
    eye = jnp.where(row == col, 1.0, 0.0)
    refs = ((qf, kf, vf, bgf, gtf, of_ref), (qb, kb, vb, bgb, gtb, ob_ref))

    chains = []
    for d, (q_ref, k_ref, v_ref, bg_ref, gt_ref, o_ref) in enumerate(refs):
        incl = (row >= col) if d == 0 else (row <= col)
        strict = (row > col) if d == 0 else (row < col)
        incl_b = jnp.where(incl, 1.0, 0.0).astype(_BF16)
        incl_t = jnp.where((row <= col) if d == 0 else (row >= col), 1.0, 0.0).astype(_BF16)
        bg = bg_ref[...]
        gc_cols = _mm_exact_lhs(incl_b, bg)
        h3 = _split3(gt_ref[0])
        gc_rows = _mm(h3[0], incl_t) + _mm(h3[1], incl_t) + _mm(h3[2], incl_t)
        total_row = c - 1 if d == 0 else 0
        for h in range(LA_HEADS):
            bcol = LA_HEADS * d + h
            gcol = 2 * LA_HEADS + LA_HEADS * d + h
            chains.append(dict(
                d=d, h=h, hs=slice(h * LA_DK, (h + 1) * LA_DK), incl=incl, strict=strict,
                q_ref=q_ref, k_ref=k_ref, v_ref=v_ref, o_ref=o_ref,
                beta=bg[:, bcol:bcol + 1], gc=gc_cols[:, gcol:gcol + 1], gc_r=gc_rows[gcol:gcol + 1, :],
                g_tot=gc_cols[total_row:total_row + 1, gcol:gcol + 1]))

    for ch in chains:
        q = ch["q_ref"][:, ch["hs"]].astype(_F32)
        k = ch["k_ref"][:, ch["hs"]]
        kf32 = k.astype(_F32)
        ch["kb"] = kf32 * ch["beta"]
        ch["egc"] = jnp.exp(ch["gc"])
        ch["a_qk"] = _mm_nt(jnp.concatenate([ch["kb"], q], axis=0).astype(_BF16), k)
        ch["qg"] = (q * ch["egc"]).astype(_BF16)
        ch["k_dec"] = (kf32 * jnp.exp(ch["g_tot"] - ch["gc"])).astype(_BF16)

    for ch in chains:
        incl = ch["incl"]
        decay = jnp.where(incl, jnp.exp(jnp.where(incl, ch["gc"] - ch["gc_r"], 0.0)), 0.0)
        l_mat = jnp.where(ch["strict"], ch["a_qk"][:c] * decay, 0.0)
        ch["qk"] = (ch["a_qk"][c:] * decay).astype(_BF16)
        ch["y"] = eye - l_mat
        l_b = l_mat.astype(_BF16)
        ch["m_b"] = _mm(l_b, l_b).astype(_BF16)
    p = 2
    while p < c:
        last = 2 * p >= c
        for ch in chains:
            lhs = ch["y"].astype(_BF16) if last else jnp.concatenate([ch["m_b"], ch["y"].astype(_BF16)], axis=0)
            prod = _mm(lhs, ch["m_b"])
            if last:
                ch["y"] = ch["y"] + prod
            else:
                ch["m_b"] = prod[:c].astype(_BF16)
                ch["y"] = ch["y"] + prod[c:]
        p *= 2

    for ch in chains:
        v = ch["v_ref"][:, ch["hs"]].astype(_F32)
        rhs = jnp.concatenate([v * ch["beta"], ch["kb"] * ch["egc"]], axis=1).astype(_BF16)
        ch["uw"] = _mm(ch["y"].astype(_BF16), rhs)
    for ch in chains:
        ch["state"] = state_ref[ch["d"], ch["h"]]
        w_b = ch["uw"][:, LA_DV:].astype(_BF16)
        ch["ws_qs"] = _mm(jnp.concatenate([w_b, ch["qg"]], axis=0), ch["state"].astype(_BF16))
    for ch in chains:
        v_new_b = (ch["uw"][:, :LA_DV] - ch["ws_qs"][:c]).astype(_BF16)
        ch["o_ref"][:, ch["hs"]] = (ch["ws_qs"][c:] + _mm(ch["qk"], v_new_b)).astype(ch["o_ref"].dtype)
        state_ref[ch["d"], ch["h"]] = ch["state"] * jnp.exp(ch["g_tot"]) + _mm_tn(ch["k_dec"], v_new_b)


def _delta_rule(q, k, v, bg, g_t, batch, seq):
    t = q.shape[0]
    c = DELTA_CHUNK
    n = seq // c
    fwd = lambda b, j: (b * n + j, 0)
    bwd = lambda b, j: (b * n + n - 1 - j, 0)
    fwd3 = lambda b, j: (b * n + j, 0, 0)
    bwd3 = lambda b, j: (b * n + n - 1 - j, 0, 0)

    def specs(im, im3):
        return [pl.BlockSpec((c, LA_QK), im), pl.BlockSpec((c, LA_QK), im), pl.BlockSpec((c, LA_V), im),
                pl.BlockSpec((c, LANES), im), pl.BlockSpec((1, 4 * LA_HEADS, c), im3)]

    out = jax.ShapeDtypeStruct((t, LA_V), _BF16)
    return pl.pallas_call(
        _delta_body,
        grid=(batch, n),
        in_specs=specs(fwd, fwd3) + specs(bwd, bwd3),
        out_specs=[pl.BlockSpec((c, LA_V), fwd), pl.BlockSpec((c, LA_V), bwd)],
        out_shape=[out, out],
        scratch_shapes=[pltpu.VMEM((2, LA_HEADS, LA_DK, LA_DV), _F32)],
        compiler_params=_params("parallel", "arbitrary"),
        name="delta_rule",
    )(q, k, v, bg, g_t, q, k, v, bg, g_t)


def _attn_body(sink_ref, q_ref, kp_ref, kc_ref, kn_ref, vp_ref, vc_ref, vn_ref, o_ref, *, seq):
    n = pl.program_id(1)
    blk = ATT_BLOCK
    group = ATT_Q_HEADS // ATT_KV_HEADS
    kw = jnp.concatenate([kp_ref[...], kc_ref[...], kn_ref[...]], axis=0)
    vw = jnp.concatenate([vp_ref[...], vc_ref[...], vn_ref[...]], axis=0)
    row = lax.broadcasted_iota(jnp.int32, (blk, 3 * blk), 0)
    col = lax.broadcasted_iota(jnp.int32, (blk, 3 * blk), 1)
    band = jnp.abs(row + blk - col) <= WINDOW

    chains = []
    for qb in range(ATT_QB):
        kpos = (ATT_QB * n + qb - 1) * blk + col
        bias = jnp.where(band & (kpos >= 0) & (kpos < seq), 0.0, ATT_MASKED)
        bias = jnp.concatenate([bias] * group, axis=0)
        for g in range(ATT_KV_HEADS):
            heads = [g * group + j for j in range(group)]
            chains.append(dict(qb=qb, g=g, heads=heads, bias=bias, rows=slice(qb * blk, (qb + 1) * blk),
                               keys=slice(qb * blk, (qb + 3) * blk), cols=slice(g * ATT_HD, (g + 1) * ATT_HD)))
    for ch in chains:
        qs = jnp.concatenate([q_ref[ch["rows"], h * ATT_HD:(h + 1) * ATT_HD] for h in ch["heads"]], axis=0)
        ch["s"] = _mm_nt(qs, kw[ch["keys"], ch["cols"]]) + ch["bias"]
    for ch in chains:
        s = ch["s"]
        sk = jnp.concatenate([jnp.full((blk, 1), sink_ref[h], _F32) for h in ch["heads"]], axis=0)
        m = jnp.maximum(jnp.max(s, axis=-1, keepdims=True), sk)
        p = jnp.exp(s - m)
        ch["denom"] = jnp.sum(p, axis=-1, keepdims=True) + jnp.exp(sk - m)
        ch["p"] = p.astype(_BF16)
    for ch in chains:
        o = _mm(ch["p"], vw[ch["keys"], ch["cols"]]) / ch["denom"]
        for j, h in enumerate(ch["heads"]):
            o_ref[ch["rows"], h * ATT_HD:(h + 1) * ATT_HD] = o[j * blk:(j + 1) * blk].astype(o_ref.dtype)


def _window_attention(aqkv, sink, batch, seq):
    t = aqkv.shape[0]
    blk = ATT_BLOCK
    nb = seq // blk
    ns = nb // ATT_QB
    kcol = ATT_Q // ATT_KV
    prev = lambda b, n: b * nb + jnp.maximum(ATT_QB * n - 1, 0)
    nxt = lambda b, n: b * nb + jnp.minimum(ATT_QB * n + ATT_QB, nb - 1)
    cur = lambda b, n: b * ns + n
    kv_specs = []
    for c in (kcol, kcol + 1):
        kv_specs += [pl.BlockSpec((blk, ATT_KV), functools.partial(lambda b, n, c: (prev(b, n), c), c=c)),
                     pl.BlockSpec((ATT_QB * blk, ATT_KV), functools.partial(lambda b, n, c: (cur(b, n), c), c=c)),
                     pl.BlockSpec((blk, ATT_KV), functools.partial(lambda b, n, c: (nxt(b, n), c), c=c))]
    return pl.pallas_call(
        functools.partial(_attn_body, seq=seq),
        grid=(batch, ns),
        in_specs=[pl.BlockSpec(memory_space=pltpu.SMEM),
                  pl.BlockSpec((ATT_QB * blk, ATT_Q), lambda b, n: (cur(b, n), 0)),
                  *kv_specs],
        out_specs=pl.BlockSpec((ATT_QB * blk, ATT_Q), lambda b, n: (cur(b, n), 0)),
        out_shape=jax.ShapeDtypeStruct((t, ATT_Q), _BF16),
        compiler_params=_params("parallel", "parallel"),
        name="window_attention",
    )(sink, *([aqkv] * 7))


def _merge_body(of_ref, ob_ref, z_ref, ga_ref, gb_ref, oatt_ref, x_ref, wa_ref, wb_ref, wo_ref,
                ng_ref, fg_ref, wr_ref, xmid_ref, hn_ref, aff_ref, oa_ref):
    o = of_ref[...].astype(_F32) + ob_ref[...].astype(_F32)
    z = z_ref[...].astype(_F32)
    gate = z * _sigmoid(z)
    for h in range(LA_HEADS):
        hs = slice(h * LA_DV, (h + 1) * LA_DV)
        oh = o[:, hs]
        oh = _rms(oh, ng_ref[...])
        oa_ref[:, hs] = (oh * gate[:, hs]).astype(_BF16)
    merged = (ga_ref[...].astype(_F32) * _mm(oa_ref[...], wa_ref[...])
              + gb_ref[...].astype(_F32) * _mm(oatt_ref[...], wb_ref[...]))
    xmid = x_ref[...] + _mm(merged.astype(_BF16), wo_ref[...])
    xmid_ref[...] = xmid
    hn = _rms(xmid, fg_ref[...])
    hn_ref[...] = hn.astype(_BF16)
    tm = hn.shape[0]
    wh, wl, _ = _split3(wr_ref[...])
    hh, hl, _ = _split3(hn)
    r1 = _mm(jnp.concatenate([hh, hl], axis=0), wh)
    logits = r1[:tm] + r1[tm:] + _mm(hh, wl)
    e = jnp.exp(logits - jnp.max(logits, axis=-1, keepdims=True))
    aff_ref[...] = e / jnp.sum(e, axis=-1, keepdims=True)


def _merge(o_f, o_b, z, gates, o_att, x, w_a, w_b, w_o, norm_g, ffn_g, w_r):
    t, d = x.shape
    tm = min(MERGE_TM, t)
    row = lambda i: (i, 0)
    fix = lambda i: (0, 0)
    act = pl.BlockSpec((tm, d), row)
    wspec = pl.BlockSpec((d, d), fix)
    return pl.pallas_call(
        _merge_body,
        grid=(t // tm,),
        in_specs=[act, act, act, act, pl.BlockSpec((tm, d), lambda i: (i, 1)), act, act,
                  wspec, wspec, wspec,
                  pl.BlockSpec((1, LA_DV), fix), pl.BlockSpec((1, d), fix),
                  pl.BlockSpec((d, N_EXPERTS), fix)],
        out_specs=[act, act, pl.BlockSpec((tm, N_EXPERTS), row)],
        out_shape=[jax.ShapeDtypeStruct((t, d), _F32), jax.ShapeDtypeStruct((t, d), _BF16),
                   jax.ShapeDtypeStruct((t, N_EXPERTS), _F32)],
        scratch_shapes=[pltpu.VMEM((tm, d), _BF16)],
        compiler_params=_params("parallel"),
        name="merge_router",
    )(o_f, o_b, z, gates, gates, o_att, x, w_a, w_b, w_o, norm_g, ffn_g, w_r)


def _select_body(aff_ref, p_ref, starts_ref, *, cap, n_blk):
    e = N_EXPERTS
    blk = SELECT_BLK

    def bits_of(x):
        return pltpu.bitcast(x, jnp.int32)

    def bisect(it, prefix):
        cand = prefix | jnp.left_shift(jnp.int32(1), 30 - it)
        cnt = jnp.sum(jnp.where(bits_of(aff_ref[...]) >= cand, 1, 0), axis=1, keepdims=True)
        return jnp.where(cnt >= cap, cand, prefix)

    thr = lax.fori_loop(0, 31, bisect, jnp.zeros((e, 1), jnp.int32))
    n_gt = jnp.sum(jnp.where(bits_of(aff_ref[...]) > thr, 1, 0), axis=1, keepdims=True)
    need = (cap - n_gt).astype(_F32)

    r = lax.broadcasted_iota(jnp.int32, (blk, blk), 0)
    c = lax.broadcasted_iota(jnp.int32, (blk, blk), 1)
    upper = jnp.where(r <= c, 1.0, 0.0).astype(_BF16)
    ones = jnp.ones((SUBLANES, blk), _BF16)

    def block(j, carry):
        eq_c, sel_c, sel_row = carry
        off = pl.multiple_of(j * blk, blk)
        b = bits_of(aff_ref[:, pl.ds(off, blk)])
        gt = b > thr
        eq = b == thr
        eq_b = jnp.where(eq, 1.0, 0.0).astype(_BF16)
        eq_incl = _mm(eq_b, upper) + eq_c
        sel = gt | (eq & (eq_incl <= need))
        sel_b = jnp.where(sel, 1.0, 0.0).astype(_BF16)
        sel_incl = _mm(sel_b, upper) + sel_c
        p_ref[:, pl.ds(off, blk)] = jnp.where(sel, sel_incl, 0.0).astype(jnp.int32)
        starts_ref[pl.ds(j, 1), :] = sel_row[0:1].astype(jnp.int32)
        return (eq_c + jnp.sum(eq_b.astype(_F32), axis=1, keepdims=True),
                sel_c + jnp.sum(sel_b.astype(_F32), axis=1, keepdims=True),
                sel_row + _mm_nt(ones, sel_b))

    zero = jnp.zeros((e, 1), _F32)
    _, _, totals = lax.fori_loop(0, n_blk, block, (zero, zero, jnp.zeros((SUBLANES, e), _F32)))
    starts_ref[n_blk:n_blk + 1, :] = totals[0:1].astype(jnp.int32)


def _select(aff, cap):
    e, t = aff.shape
    n_blk = t // SELECT_BLK
    return pl.pallas_call(
        functools.partial(_select_body, cap=cap, n_blk=n_blk),
        out_shape=[jax.ShapeDtypeStruct((e, t), jnp.int32), jax.ShapeDtypeStruct((n_blk + 1, e), jnp.int32)],
        compiler_params=pltpu.CompilerParams(vmem_limit_bytes=VMEM_LIMIT),
        name="expert_select",
    )(aff)


def _dispatch_body(starts_ref, hn_ref, p_ref, o_ref, *, cap, blocks_per_win):
    eb = pl.program_id(0)
    i = pl.program_id(1)
    tt = DISPATCH_TT
    stride = blocks_per_win * N_EXPERTS
    widths = (min(DISPATCH_WS, cap), min(DISPATCH_W, cap))

    @pl.when(i == 0)
    def _():
        o_ref[...] = jnp.zeros_like(o_ref)

    offs = {}
    all_short = None
    for s in range(DISPATCH_SUB):
        win = i * DISPATCH_SUB + s
        for j in range(DISPATCH_EP):
            e = eb * DISPATCH_EP + j
            start = starts_ref[win * stride + e]
            end = starts_ref[(win + 1) * stride + e]
            aligned = (start // BF16_ROWS) * BF16_ROWS
            offs[s, j] = [pl.multiple_of(jnp.minimum(aligned, cap - w), BF16_ROWS) for w in widths]
            short = end - offs[s, j][0] <= widths[0]
            all_short = short if all_short is None else jnp.logical_and(all_short, short)

    def gather(k):
        w = widths[k]
        slot = lax.broadcasted_iota(jnp.int32, (w, tt), 0)
        for s in range(DISPATCH_SUB):
            onehots = []
            for j in range(DISPATCH_EP):
                e = eb * DISPATCH_EP + j
                rel = p_ref[pl.ds(e, 1), s * tt:(s + 1) * tt] - 1 - offs[s, j][k]
                onehots.append(jnp.where(rel == slot, 1.0, 0.0).astype(_BF16))
            rows = _mm(jnp.concatenate(onehots, axis=0), hn_ref[s * tt:(s + 1) * tt, :]).astype(o_ref.dtype)
            for j in range(DISPATCH_EP):
                dst = (j, pl.ds(offs[s, j][k], w), slice(None))
                o_ref[dst] = o_ref[dst] + rows[j * w:(j + 1) * w]

    pl.when(all_short)(functools.partial(gather, 0))
    pl.when(jnp.logical_not(all_short))(functools.partial(gather, 1))


def _dispatch(starts, hn, p, cap):
    t, d = hn.shape
    tb = DISPATCH_TT * DISPATCH_SUB
    ep = DISPATCH_EP
    body = functools.partial(_dispatch_body, cap=cap, blocks_per_win=DISPATCH_TT // SELECT_BLK)
    return pl.pallas_call(
        body,
        grid_spec=pltpu.PrefetchScalarGridSpec(
            num_scalar_prefetch=1,
            grid=(N_EXPERTS // ep, t // tb),
            in_specs=[pl.BlockSpec((tb, d), lambda e, i, s: (i, 0)),
                      pl.BlockSpec((N_EXPERTS, tb), lambda e, i, s: (0, i))],
            out_specs=pl.BlockSpec((ep, cap, d), lambda e, i, s: (e, 0, 0))),
        out_shape=jax.ShapeDtypeStruct((N_EXPERTS, cap, d), _BF16),
        compiler_params=_params("parallel", "arbitrary"),
        name="expert_dispatch",
    )(starts, hn, p)


def _ffn_body(x_ref, wg_ref, wu_ref, wd_ref, o_ref, acc_ref, *, tm):
    fc = pl.program_id(1)

    @pl.when(fc == 0)
    def _():
        acc_ref[...] = jnp.zeros_like(acc_ref)

    wg = wg_ref[0, 0].astype(_BF16)
    wu = wu_ref[0, 0].astype(_BF16)
    wd = wd_ref[0, 0].astype(_BF16)
    for r in range(x_ref.shape[1] // tm):
        rows = slice(r * tm, (r + 1) * tm)
        x = x_ref[0, rows, :]
        a = _mm(x, wg)
        b = _mm(x, wu)
        hid = (a * _sigmoid(a) * b).astype(_BF16)
        acc_ref[rows, :] += _mm(hid, wd)

    @pl.when(fc == pl.num_programs(1) - 1)
    def _():
        o_ref[0] = acc_ref[...].astype(o_ref.dtype)


def _expert_ffn(xe, w_gate, w_up, w_down, layer):
    e, cap, d = xe.shape
    ff = w_gate.shape[3]
    fch = min(FFN_CHUNK, ff)
    tm = min(FFN_TM, cap)
    return pl.pallas_call(
        functools.partial(_ffn_body, tm=tm),
        grid=(e, ff // fch),
        in_specs=[pl.BlockSpec((1, cap, d), lambda e, c: (e, 0, 0)),
                  pl.BlockSpec((1, 1, d, fch), lambda e, c: (layer, e, 0, c)),
                  pl.BlockSpec((1, 1, d, fch), lambda e, c: (layer, e, 0, c)),
                  pl.BlockSpec((1, 1, fch, d), lambda e, c: (layer, e, c, 0))],
        out_specs=pl.BlockSpec((1, cap, d), lambda e, c: (e, 0, 0)),
        out_shape=jax.ShapeDtypeStruct((e, cap, d), _BF16),
        scratch_shapes=[pltpu.VMEM((cap, d), _F32)],
        compiler_params=_params("parallel", "arbitrary"),
        name="expert_ffn",
    )(xe, w_gate, w_up, w_down)


def _combine_body(starts_ref, xmid_ref, pt_ref, afft_ref, g_ref, ye_ref, o_ref, sbuf_ref, bbuf_ref, sem_ref,
                  *, cap, final_norm, ws, wl):
    i = pl.program_id(0)
    n = pl.num_programs(0)
    tt = COMBINE_TT
    stride = (tt // SELECT_BLK) * N_EXPERTS

    def windows(tile, slot):
        per_expert = []
        all_short = None
        for e in range(N_EXPERTS):
            start = starts_ref[tile * stride + e]
            end = starts_ref[(tile + 1) * stride + e]
            aligned = (start // BF16_ROWS) * BF16_ROWS
            pair = []
            for w, buf in ((ws, sbuf_ref), (wl, bbuf_ref)):
                a0 = pl.multiple_of(jnp.minimum(aligned, cap - w), BF16_ROWS)
                copy = pltpu.make_async_copy(ye_ref.at[e, pl.ds(a0, w), :], buf.at[slot, pl.ds(e * w, w), :],
                                             sem_ref.at[slot, e])
                pair.append((a0, copy))
            per_expert.append(pair)
            short = end - pair[0][0] <= ws
            all_short = short if all_short is None else jnp.logical_and(all_short, short)
        return all_short, per_expert

    def start_tile(tile, slot):
        all_short, per_expert = windows(tile, slot)
        for k in range(2):
            @pl.when(all_short if k == 0 else jnp.logical_not(all_short))
            def _():
                for pair in per_expert:
                    pair[k][1].start()

    @pl.when(i == 0)
    def _():
        start_tile(0, 0)

    @pl.when(i + 1 < n)
    def _():
        start_tile(i + 1, lax.rem(i + 1, 2))

    slot = lax.rem(i, 2)
    all_short, per_expert = windows(i, slot)

    def expand(k, w, buf):
        group = max(1, LANES // w)
        width = group * w
        lane = lax.broadcasted_iota(jnp.int32, (tt, width), 1)
        his, los = [], []
        for e0 in range(0, N_EXPERTS, group):
            rel = gate = None
            for j in range(group):
                e = e0 + j
                per_expert[e][k][1].wait()
                rel_e = pt_ref[:, e:e + 1] - 1 - per_expert[e][k][0]
                gate_e = afft_ref[:, e:e + 1]
                rel = rel_e if j == 0 else jnp.where(lane >= j * w, rel_e, rel)
                gate = gate_e if j == 0 else jnp.where(lane >= j * w, gate_e, gate)
            slot_of_lane = lane if group == 1 else lane & (w - 1)
            piece = jnp.where(rel == slot_of_lane, gate, 0.0)
            hi = piece.astype(_BF16)
            his.append(hi)
            los.append((piece - hi.astype(_F32)).astype(_BF16))
        lhs = jnp.concatenate([jnp.concatenate(his, axis=1), jnp.concatenate(los, axis=1)], axis=0)
        res = _mm(lhs, buf[slot])
        acc = xmid_ref[...] + res[:tt] + res[tt:]
        o_ref[...] = _rms(acc, g_ref[...]) if final_norm else acc

    pl.when(all_short)(functools.partial(expand, 0, ws, sbuf_ref))
    pl.when(jnp.logical_not(all_short))(functools.partial(expand, 1, wl, bbuf_ref))


def _combine(starts, xmid, p_t, aff_t, g_final, ye, cap, final_norm):
    t, d = xmid.shape
    tt = COMBINE_TT
    ws, wl = min(COMBINE_WS, cap), min(COMBINE_W, cap)
    return pl.pallas_call(
        functools.partial(_combine_body, cap=cap, final_norm=final_norm, ws=ws, wl=wl),
        grid_spec=pltpu.PrefetchScalarGridSpec(
            num_scalar_prefetch=1,
            grid=(t // tt,),
            in_specs=[pl.BlockSpec((tt, d), lambda i, s: (i, 0)),
                      pl.BlockSpec((tt, N_EXPERTS), lambda i, s: (i, 0)),
                      pl.BlockSpec((tt, N_EXPERTS), lambda i, s: (i, 0)),
                      pl.BlockSpec((1, d), lambda i, s: (0, 0)),
                      pl.BlockSpec(memory_space=pl.ANY)],
            out_specs=pl.BlockSpec((tt, d), lambda i, s: (i, 0)),
            scratch_shapes=[pltpu.VMEM((2, N_EXPERTS * ws, d), _BF16),
                            pltpu.VMEM((2, N_EXPERTS * wl, d), _BF16),
                            pltpu.SemaphoreType.DMA((2, N_EXPERTS))]),
        out_shape=jax.ShapeDtypeStruct((t, d), _F32),
        compiler_params=_params("arbitrary"),
        name="expert_combine",
    )(starts, xmid, p_t, aff_t, g_final, ye)


def _rope_tables(seq):
    half = ATT_HD // 2
    inv_freq = ROPE_THETA ** (-jnp.arange(half, dtype=_F32) / half)
    ang = jnp.arange(seq, dtype=_F32)[:, None] * inv_freq[None, :]
    cos, sin = jnp.cos(ang), jnp.sin(ang)
    return jnp.concatenate([cos, cos], axis=-1), jnp.concatenate([-sin, sin], axis=-1)


def _layer(x, batch, seq, w, final_g, final_norm):
    t, d = x.shape
    q, k, v, z, gates, aqkv, bg = _project(x, w, seq)
    g_t = bg[:, :4 * LA_HEADS].reshape(t // DELTA_CHUNK, DELTA_CHUNK, 4 * LA_HEADS).transpose(0, 2, 1)
    o_f, o_b = _delta_rule(q, k, v, bg, g_t, batch, seq)
    o_att = _window_attention(aqkv, w["sink"], batch, seq)
    xmid, hn, aff_t = _merge(o_f, o_b, z, gates, o_att, x, w["w_a"], w["w_b"], w["w_o"],
                             w["la_norm_g"], w["norm_ffn_g"], w["w_r"])

    cap = max(1, CAPACITY_FACTOR * t // N_EXPERTS)
    assert cap % LANES == 0 and cap % min(FFN_TM, cap) == 0
    p, starts = _select(aff_t.T, cap)
    starts = starts.reshape(-1)
    xe = _dispatch(starts, hn, p, cap)
    ye = _expert_ffn(xe, w["w_gate"], w["w_up"], w["w_down"], w["layer"])
    return _combine(starts, xmid, p.T, aff_t, final_g, ye, cap, final_norm)


def _layer_weights(l, seq, norm_mix_g, w_in, conv_w, la_a_log, la_dt_bias, la_norm_g, attn_sink, w_branch_a,
                   w_branch_b, w_out, norm_ffn_g, w_router, w_exp_gate, w_exp_up, w_exp_down):
    wi = w_in[l]
    c0 = CONV_CH
    c1 = c0 + LA_V
    c2 = c1 + 4 * LA_HEADS
    c3 = c2 + ATT_COLS
    pad = LANES - 4 * LA_HEADS
    zeros = jnp.zeros((2 * LA_HEADS,), _F32)
    lane_pad = lambda v: jnp.pad(jnp.concatenate([zeros, v.reshape(-1)]), (0, pad)).reshape(1, LANES)
    return {
        "norm_mix_g": norm_mix_g[l].reshape(1, -1),
        "w_conv": wi[:, :c0].astype(_BF16),
        "w_z": wi[:, c0:c1].astype(_BF16),
        "w_bg": jnp.pad(wi[:, c1:c2], ((0, 0), (0, pad))).astype(_BF16),
        "w_att": wi[:, c2:c3].astype(_BF16),
        "w_gates": wi[:, c3:].astype(_BF16),
        "a_log": lane_pad(la_a_log[l]),
        "dt_bias": lane_pad(la_dt_bias[l]),
        "rope": _rope_tables(seq),
        "conv_w": conv_w[l],
        "sink": attn_sink[l],
        "w_a": w_branch_a[l].astype(_BF16),
        "w_b": w_branch_b[l].astype(_BF16),
        "w_o": w_out[l].astype(_BF16),
        "la_norm_g": la_norm_g[l].reshape(1, -1),
        "norm_ffn_g": norm_ffn_g[l].reshape(1, -1),
        "w_r": w_router[l],
        "layer": l,
        "w_gate": w_exp_gate,
        "w_up": w_exp_up,
        "w_down": w_exp_down,
    }


def kernel(x_prompt, x_sample, norm_mix_g, w_in, conv_w, la_a_log, la_dt_bias, la_norm_g, attn_sink, w_branch_a, w_branch_b, w_out, norm_ffn_g, w_router, w_exp_gate, w_exp_up, w_exp_down, final_norm_g):
    depth = w_in.shape[0]
    layer_args = (norm_mix_g, w_in, conv_w, la_a_log, la_dt_bias, la_norm_g, attn_sink, w_branch_a, w_branch_b,
                  w_out, norm_ffn_g, w_router, w_exp_gate, w_exp_up, w_exp_down)
    final_g = final_norm_g.reshape(1, -1)

    def trunk(x):
        batch, seq, d = x.shape
        h = x.reshape(batch * seq, d)
        for l in range(depth):
            w = _layer_weights(l, seq, *layer_args)
            h = _layer(h, batch, seq, w, final_g, l == depth - 1)
        return h.reshape(batch, seq, d)

    return trunk(x_prompt), trunk(x_sample)
```

```python
import functools

import jax
import jax.numpy as jnp
from jax import lax
from jax.experimental import pallas as pl
from jax.experimental.pallas import tpu as pltpu

D_MODEL = 1024
LA_HEADS = 8
LA_DK = 128
LA_DV = 128
CONV_W = 4
ATT_Q_HEADS = 8
ATT_KV_HEADS = 2
ATT_HD = 128
WINDOW = 128
ROPE_THETA = 10000.0
N_EXPERTS = 16
EXPERT_FF = 2048
CAPACITY_FACTOR = 2
EPS = 1e-6

LA_QK = LA_HEADS * LA_DK
LA_V = LA_HEADS * LA_DV
ATT_Q = ATT_Q_HEADS * ATT_HD
ATT_KV = ATT_KV_HEADS * ATT_HD
CONV_CH = 2 * LA_QK + LA_V
ATT_COLS = ATT_Q + 2 * ATT_KV

LANES = 128
SUBLANES = 8
BF16_ROWS = 16
VMEM_LIMIT = 56 * 1024 * 1024

DELTA_CHUNK = 128
ATT_BLOCK = 128
ATT_QB = 16
ATT_MASKED = -1e30
PROJ_TM = 512
MERGE_TM = 512
DISPATCH_TT = 256
DISPATCH_SUB = 4
DISPATCH_EP = 4
DISPATCH_W = DISPATCH_TT + BF16_ROWS
DISPATCH_WS = 64
COMBINE_TT = 256
COMBINE_W = 384
COMBINE_WS = 64
FFN_TM = 1024
FFN_CHUNK = 512
SELECT_BLK = 128
SELECT_GROUP = 4

_BF16 = jnp.bfloat16
_F32 = jnp.float32


def _params(*sem):
    return pltpu.CompilerParams(dimension_semantics=sem, vmem_limit_bytes=VMEM_LIMIT)


def _rms(x, g):
    return x * lax.rsqrt(jnp.mean(x * x, axis=-1, keepdims=True) + EPS) * g


def _sigmoid(x):
    return 1.0 / (1.0 + jnp.exp(-x))


def _mm(a, b):
    return jnp.dot(a, b, preferred_element_type=_F32)


def _mm_nt(a, b):
    return lax.dot_general(a, b, (((1,), (1,)), ((), ())), preferred_element_type=_F32)


def _mm_tn(a, b):
    return lax.dot_general(a, b, (((0,), (0,)), ((), ())), preferred_element_type=_F32)


def _split3(x):
    h = x.astype(_BF16)
    r = x - h.astype(_F32)
    m = r.astype(_BF16)
    l = (r - m.astype(_F32)).astype(_BF16)
    return h, m, l


def _mm_exact_lhs(a01, x):
    h, m, l = _split3(x)
    return _mm(a01, h) + _mm(a01, m) + _mm(a01, l)


def _proj_body(x_ref, xp_ref, xn_ref, g_ref, wc_ref, wz_ref, wg_ref, wa_ref, wb_ref, cw_ref, alog_ref, dt_ref,
               cos_ref, sin_ref, q_ref, k_ref, v_ref, z_ref, gates_ref, aqkv_ref, bg_ref, hn_ref, ext_ref,
               *, tiles_per_seq, tm):
    i = pl.program_id(0)
    pos = lax.rem(i, tiles_per_seq)
    keep_prev = jnp.where(pos == 0, 0.0, 1.0)
    keep_next = jnp.where(pos == tiles_per_seq - 1, 0.0, 1.0)
    hr = BF16_ROWS
    g = g_ref[...]
    hn_ref[0:hr, :] = _rms(xp_ref[...], g).astype(_BF16)
    hn_ref[hr:hr + tm, :] = _rms(x_ref[...], g).astype(_BF16)
    hn_ref[hr + tm:, :] = _rms(xn_ref[...], g).astype(_BF16)
    hn = hn_ref[hr:hr + tm, :]

    for c, o_ref in enumerate((q_ref, k_ref, v_ref)):
        sl = slice(c * LA_QK, (c + 1) * LA_QK)
        raw = _mm(hn_ref[...], wc_ref[:, sl])
        ext_ref[0:hr, :] = raw[0:hr] * keep_prev
        ext_ref[hr:hr + tm, :] = raw[hr:hr + tm]
        ext_ref[hr + tm:, :] = raw[hr + tm:] * keep_next
        w = cw_ref[:, sl]
        ext = ext_ref[...]
        rows = ext.shape[0]
        taps = [pltpu.roll(ext, 2, 0), pltpu.roll(ext, 1, 0), ext, pltpu.roll(ext, rows - 1, 0)]
        y = sum(tap[hr:hr + tm] * w[j:j + 1] for j, tap in enumerate(taps))
        y = y * _sigmoid(y)
        if c == 2:
            o_ref[...] = y.astype(o_ref.dtype)
            continue
        scale = LA_DK ** -0.5 if c == 0 else 1.0
        for h in range(LA_HEADS):
            hs = slice(h * LA_DK, (h + 1) * LA_DK)
            yh = y[:, hs]
            yh = yh * lax.rsqrt(jnp.sum(yh * yh, axis=-1, keepdims=True) + EPS)
            o_ref[:, hs] = (yh * scale).astype(o_ref.dtype)

    z_ref[...] = _mm(hn, wz_ref[...]).astype(z_ref.dtype)
    for c in range(2):
        sl = slice(c * D_MODEL, (c + 1) * D_MODEL)
        gates_ref[:, sl] = _sigmoid(_mm(hn, wg_ref[:, sl])).astype(gates_ref.dtype)

    acc = _mm(hn, wa_ref[...])
    cos, sin = cos_ref[...], sin_ref[...]
    qcos, qsin = cos * (ATT_HD ** -0.5), sin * (ATT_HD ** -0.5)
    for h in range(ATT_Q_HEADS + ATT_KV_HEADS):
        sl = slice(h * ATT_HD, (h + 1) * ATT_HD)
        xh = acc[:, sl]
        c, s = (qcos, qsin) if h < ATT_Q_HEADS else (cos, sin)
        aqkv_ref[:, sl] = (xh * c + pltpu.roll(xh, ATT_HD // 2, 1) * s).astype(aqkv_ref.dtype)
    aqkv_ref[:, ATT_Q + ATT_KV:] = acc[:, ATT_Q + ATT_KV:].astype(aqkv_ref.dtype)

    acc = _mm(hn, wb_ref[...])
    zz = acc + dt_ref[...]
    sp = jnp.maximum(zz, 0.0) + jnp.log1p(jnp.exp(-jnp.abs(zz)))
    lane = lax.broadcasted_iota(jnp.int32, acc.shape, 1)
    decay = jnp.where(lane < 4 * LA_HEADS, -jnp.exp(alog_ref[...]) * sp, 0.0)
    bg_ref[...] = jnp.where(lane < 2 * LA_HEADS, _sigmoid(acc), decay)


def _project(x, w, seq):
    t, d = x.shape
    tm = min(PROJ_TM, seq)
    halo_blocks = tm // BF16_ROWS
    n_halo = t // BF16_ROWS
    tiles_per_seq = seq // tm
    row = lambda i: (i, 0)
    fix = lambda i: (0, 0)
    resident = lambda shape: pl.BlockSpec(shape, fix, pipeline_mode=pl.Buffered(1))
    rope_spec = pl.BlockSpec((tm, ATT_HD), lambda i: (lax.rem(i, tiles_per_seq), 0))
    bf = lambda n: jax.ShapeDtypeStruct((t, n), _BF16)
    return pl.pallas_call(
        functools.partial(_proj_body, tiles_per_seq=tiles_per_seq, tm=tm),
        grid=(t // tm,),
        in_specs=[pl.BlockSpec((tm, d), row),
                  pl.BlockSpec((BF16_ROWS, d), lambda i: (jnp.maximum(i * halo_blocks - 1, 0), 0)),
                  pl.BlockSpec((BF16_ROWS, d), lambda i: (jnp.minimum((i + 1) * halo_blocks, n_halo - 1), 0)),
                  resident((1, d)),
                  resident((d, CONV_CH)), resident((d, LA_V)), resident((d, 2 * D_MODEL)),
                  resident((d, ATT_COLS)), resident((d, LANES)),
                  resident((CONV_W, CONV_CH)), resident((1, LANES)), resident((1, LANES)),
                  rope_spec, rope_spec],
        out_specs=[pl.BlockSpec((tm, LA_QK), row), pl.BlockSpec((tm, LA_QK), row), pl.BlockSpec((tm, LA_V), row),
                   pl.BlockSpec((tm, LA_V), row), pl.BlockSpec((tm, 2 * D_MODEL), row),
                   pl.BlockSpec((tm, ATT_COLS), row), pl.BlockSpec((tm, LANES), row)],
        out_shape=[bf(LA_QK), bf(LA_QK), bf(LA_V), bf(LA_V), bf(2 * D_MODEL), bf(ATT_COLS),
                   jax.ShapeDtypeStruct((t, LANES), _F32)],
        scratch_shapes=[pltpu.VMEM((tm + 2 * BF16_ROWS, d), _BF16),
                        pltpu.VMEM((tm + 2 * BF16_ROWS, LA_QK), _F32)],
        compiler_params=_params("parallel"),
        name="in_proj",
    )(x, x, x, w["norm_mix_g"], w["w_conv"], w["w_z"], w["w_gates"], w["w_att"], w["w_bg"],
      w["conv_w"], w["a_log"], w["dt_bias"], *w["rope"])


def _delta_body(qf, kf, vf, bgf, gtf, qb, kb, vb, bgb, gtb, of_ref, ob_ref, state_ref):
    @pl.when(pl.program_id(1) == 0)
    def _():
        state_ref[...] = jnp.zeros_like(state_ref)

    c = DELTA_CHUNK
    row = lax.broadcasted_iota(jnp.int32, (c, c), 0)
    col = lax.broadcasted_iota(jnp.int32, (c, c), 1)
    eye = jnp.where(row == col, 1.0, 0.0)
    refs = ((qf, kf, vf, bgf, gtf, of_ref), (qb, kb, vb, bgb, gtb, ob_ref))

    chains = []
    for d, (q_ref, k_ref, v_ref, bg_ref, gt_ref, o_ref) in enumerate(refs):
        incl = (row >= col) if d == 0 else (row <= col)
        strict = (row > col) if d == 0 else (row < col)
        incl_b = jnp.where(incl, 1.0, 0.0).astype(_BF16)
        incl_t = jnp.where((row <= col) if d == 0 else (row >= col), 1.0, 0.0).astype(_BF16)
        bg = bg_ref[...]
        gc_cols = _mm_exact_lhs(incl_b, bg)
        h3 = _split3(gt_ref[0])
        gc_rows = _mm(h3[0], incl_t) + _mm(h3[1], incl_t) + _mm(h3[2], incl_t)
        total_row = c - 1 if d == 0 else 0
        for h in range(LA_HEADS):
            bcol = LA_HEADS * d + h
            gcol = 2 * LA_HEADS + LA_HEADS * d + h
            chains.append(dict(
                d=d, h=h, hs=slice(h * LA_DK, (h + 1) * LA_DK), incl=incl, strict=strict,
                q_ref=q_ref, k_ref=k_ref, v_ref=v_ref, o_ref=o_ref,
                beta=bg[:, bcol:bcol + 1], gc=gc_cols[:, gcol:gcol + 1], gc_r=gc_rows[gcol:gcol + 1, :],
                g_tot=gc_cols[total_row:total_row + 1, gcol:gcol + 1]))

    for ch in chains:
        q = ch["q_ref"][:, ch["hs"]].astype(_F32)
        k = ch["k_ref"][:, ch["hs"]]
        kf32 = k.astype(_F32)
        ch["kb"] = kf32 * ch["beta"]
        ch["egc"] = jnp.exp(ch["gc"])
        ch["a_qk"] = _mm_nt(jnp.concatenate([ch["kb"], q], axis=0).astype(_BF16), k)
        ch["qg"] = (q * ch["egc"]).astype(_BF16)
        ch["k_dec"] = (kf32 * jnp.exp(ch["g_tot"] - ch["gc"])).astype(_BF16)

    for ch in chains:
        incl = ch["incl"]
        decay = jnp.where(incl, jnp.exp(jnp.where(incl, ch["gc"] - ch["gc_r"], 0.0)), 0.0)
        l_mat = jnp.where(ch["strict"], ch["a_qk"][:c] * decay, 0.0)
        ch["qk"] = (ch["a_qk"][c:] * decay).astype(_BF16)
        ch["y"] = eye - l_mat
        l_b = l_mat.astype(_BF16)
        ch["m_b"] = _mm(l_b, l_b).astype(_BF16)
    p = 2
    while p < c:
        last = 2 * p >= c
        for ch in chains:
            lhs = ch["y"].astype(_BF16) if last else jnp.concatenate([ch["m_b"], ch["y"].astype(_BF16)], axis=0)
            prod = _mm(lhs, ch["m_b"])
            if last:
                ch["y"] = ch["y"] + prod
            else:
                ch["m_b"] = prod[:c].astype(_BF16)
                ch["y"] = ch["y"] + prod[c:]
        p *= 2

    for ch in chains:
        v = ch["v_ref"][:, ch["hs"]].astype(_F32)
        rhs = jnp.concatenate([v * ch["beta"], ch["kb"] * ch["egc"]], axis=1).astype(_BF16)
        ch["uw"] = _mm(ch["y"].astype(_BF16), rhs)
    for ch in chains:
        ch["state"] = state_ref[ch["d"], ch["h"]]
        w_b = ch["uw"][:, LA_DV:].astype(_BF16)
        ch["ws_qs"] = _mm(jnp.concatenate([w_b, ch["qg"]], axis=0), ch["state"].astype(_BF16))
    for ch in chains:
        v_new_b = (ch["uw"][:, :LA_DV] - ch["ws_qs"][:c]).astype(_BF16)
        ch["o_ref"][:, ch["hs"]] = (ch["ws_qs"][c:] + _mm(ch["qk"], v_new_b)).astype(ch["o_ref"].dtype)
        state_ref[ch["d"], ch["h"]] = ch["state"] * jnp.exp(ch["g_tot"]) + _mm_tn(ch["k_dec"], v_new_b)


def _delta_rule(q, k, v, bg, g_t, batch, seq):
    t = q.shape[0]
    c = DELTA_CHUNK
    n = seq // c
    fwd = lambda b, j: (b * n + j, 0)
    bwd = lambda b, j: (b * n + n - 1 - j, 0)
    fwd3 = lambda b, j: (b * n + j, 0, 0)
    bwd3 = lambda b, j: (b * n + n - 1 - j, 0, 0)

    def specs(im, im3):
        return [pl.BlockSpec((c, LA_QK), im), pl.BlockSpec((c, LA_QK), im), pl.BlockSpec((c, LA_V), im),
                pl.BlockSpec((c, LANES), im), pl.BlockSpec((1, 4 * LA_HEADS, c), im3)]

    out = jax.ShapeDtypeStruct((t, LA_V), _F32)
    return pl.pallas_call(
        _delta_body,
        grid=(batch, n),
        in_specs=specs(fwd, fwd3) + specs(bwd, bwd3),
        out_specs=[pl.BlockSpec((c, LA_V), fwd), pl.BlockSpec((c, LA_V), bwd)],
        out_shape=[out, out],
        scratch_shapes=[pltpu.VMEM((2, LA_HEADS, LA_DK, LA_DV), _F32)],
        compiler_params=_params("parallel", "arbitrary"),
        name="delta_rule",
    )(q, k, v, bg, g_t, q, k, v, bg, g_t)


def _attn_body(sink_ref, q_ref, kp_ref, kc_ref, kn_ref, vp_ref, vc_ref, vn_ref, o_ref, *, seq, nq):
    n = pl.program_id(1)
    blk = ATT_BLOCK
    group = ATT_Q_HEADS // ATT_KV_HEADS
    kw = jnp.concatenate([kp_ref[...], kc_ref[...], kn_ref[...]], axis=0)
    vw = jnp.concatenate([vp_ref[...], vc_ref[...], vn_ref[...]], axis=0)
    row = lax.broadcasted_iota(jnp.int32, (blk, 3 * blk), 0)
    col = lax.broadcasted_iota(jnp.int32, (blk, 3 * blk), 1)
    band = jnp.abs(row + blk - col) <= WINDOW

    chains = []
    for qb in range(nq):
        kpos = (nq * n + qb - 1) * blk + col
        bias = jnp.where(band & (kpos >= 0) & (kpos < seq), 0.0, ATT_MASKED)
        bias = jnp.concatenate([bias] * group, axis=0)
        for g in range(ATT_KV_HEADS):
            heads = [g * group + j for j in range(group)]
            chains.append(dict(qb=qb, g=g, heads=heads, bias=bias, rows=slice(qb * blk, (qb + 1) * blk),
                               keys=slice(qb * blk, (qb + 3) * blk), cols=slice(g * ATT_HD, (g + 1) * ATT_HD)))
    for ch in chains:
        qs = jnp.concatenate([q_ref[ch["rows"], h * ATT_HD:(h + 1) * ATT_HD] for h in ch["heads"]], axis=0)
        ch["s"] = _mm_nt(qs, kw[ch["keys"], ch["cols"]]) + ch["bias"]
    for ch in chains:
        s = ch["s"]
        sk = jnp.concatenate([jnp.full((blk, 1), sink_ref[h], _F32) for h in ch["heads"]], axis=0)
        m = jnp.maximum(jnp.max(s, axis=-1, keepdims=True), sk)
        p = jnp.exp(s - m)
        ch["denom"] = jnp.sum(p, axis=-1, keepdims=True) + jnp.exp(sk - m)
        ch["p"] = p.astype(_BF16)
    for ch in chains:
        o = _mm(ch["p"], vw[ch["keys"], ch["cols"]]) / ch["denom"]
        for j, h in enumerate(ch["heads"]):
            o_ref[ch["rows"], h * ATT_HD:(h + 1) * ATT_HD] = o[j * blk:(j + 1) * blk].astype(o_ref.dtype)


def _window_attention(aqkv, sink, batch, seq):
    t = aqkv.shape[0]
    blk = ATT_BLOCK
    nb = seq // blk
    nq = ATT_QB if nb % ATT_QB == 0 else 1
    ns = nb // nq
    kcol = ATT_Q // ATT_KV
    prev = lambda b, n: b * nb + jnp.maximum(nq * n - 1, 0)
    nxt = lambda b, n: b * nb + jnp.minimum(nq * n + nq, nb - 1)
    cur = lambda b, n: b * ns + n
    kv_specs = []
    for c in (kcol, kcol + 1):
        kv_specs += [pl.BlockSpec((blk, ATT_KV), functools.partial(lambda b, n, c: (prev(b, n), c), c=c)),
                     pl.BlockSpec((nq * blk, ATT_KV), functools.partial(lambda b, n, c: (cur(b, n), c), c=c)),
                     pl.BlockSpec((blk, ATT_KV), functools.partial(lambda b, n, c: (nxt(b, n), c), c=c))]
    return pl.pallas_call(
        functools.partial(_attn_body, seq=seq, nq=nq),
        grid=(batch, ns),
        in_specs=[pl.BlockSpec(memory_space=pltpu.SMEM),
                  pl.BlockSpec((nq * blk, ATT_Q), lambda b, n: (cur(b, n), 0)),
                  *kv_specs],
        out_specs=pl.BlockSpec((nq * blk, ATT_Q), lambda b, n: (cur(b, n), 0)),
        out_shape=jax.ShapeDtypeStruct((t, ATT_Q), _BF16),
        compiler_params=_params("parallel", "parallel"),
        name="window_attention",
    )(sink, *([aqkv] * 7))


def _merge_body(of_ref, ob_ref, z_ref, ga_ref, gb_ref, oatt_ref, x_ref, wa_ref, wb_ref, wo_ref,
                ng_ref, fg_ref, wr_ref, xmid_ref, hn_ref, aff_ref, oa_ref):
    o = of_ref[...].astype(_F32) + ob_ref[...].astype(_F32)
    z = z_ref[...].astype(_F32)
    gate = z * _sigmoid(z)
    for h in range(LA_HEADS):
        hs = slice(h * LA_DV, (h + 1) * LA_DV)
        oh = o[:, hs]
        oh = _rms(oh, ng_ref[...])
        oa_ref[:, hs] = (oh * gate[:, hs]).astype(_BF16)
    merged = (ga_ref[...].astype(_F32) * _mm(oa_ref[...], wa_ref[...])
              + gb_ref[...].astype(_F32) * _mm(oatt_ref[...], wb_ref[...]))
    xmid = x_ref[...] + _mm(merged.astype(_BF16), wo_ref[...])
    xmid_ref[...] = xmid
    hn = _rms(xmid, fg_ref[...])
    hn_ref[...] = hn.astype(_BF16)
    tm = hn.shape[0]
    wh, wl, _ = _split3(wr_ref[...])
    hh, hl, _ = _split3(hn)
    r1 = _mm(jnp.concatenate([hh, hl], axis=0), wh)
    logits = r1[:tm] + r1[tm:] + _mm(hh, wl)
    e = jnp.exp(logits - jnp.max(logits, axis=-1, keepdims=True))
    aff_ref[...] = e / jnp.sum(e, axis=-1, keepdims=True)


def _merge(o_f, o_b, z, gates, o_att, x, w_a, w_b, w_o, norm_g, ffn_g, w_r):
    t, d = x.shape
    tm = min(MERGE_TM, t)
    row = lambda i: (i, 0)
    fix = lambda i: (0, 0)
    act = pl.BlockSpec((tm, d), row)
    wspec = pl.BlockSpec((d, d), fix)
    return pl.pallas_call(
        _merge_body,
        grid=(t // tm,),
        in_specs=[act, act, act, act, pl.BlockSpec((tm, d), lambda i: (i, 1)), act, act,
                  wspec, wspec, wspec,
                  pl.BlockSpec((1, LA_DV), fix), pl.BlockSpec((1, d), fix),
                  pl.BlockSpec((d, N_EXPERTS), fix)],
        out_specs=[act, act, pl.BlockSpec((tm, N_EXPERTS), row)],
        out_shape=[jax.ShapeDtypeStruct((t, d), _F32), jax.ShapeDtypeStruct((t, d), _BF16),
                   jax.ShapeDtypeStruct((t, N_EXPERTS), _F32)],
        scratch_shapes=[pltpu.VMEM((tm, d), _BF16)],
        compiler_params=_params("parallel"),
        name="merge_router",
    )(o_f, o_b, z, gates, gates, o_att, x, w_a, w_b, w_o, norm_g, ffn_g, w_r)


def _select_body(aff_ref, p_ref, starts_ref, *, cap, n_blk):
    e = N_EXPERTS
    blk = SELECT_BLK

    def bits_of(x):
        return pltpu.bitcast(x, jnp.int32)

    def bisect(it, prefix):
        cand = prefix | jnp.left_shift(jnp.int32(1), 30 - it)
        cnt = jnp.sum(jnp.where(bits_of(aff_ref[...]) >= cand, 1, 0), axis=1, keepdims=True)
        return jnp.where(cnt >= cap, cand, prefix)

    thr = lax.fori_loop(0, 31, bisect, jnp.zeros((e, 1), jnp.int32))
    n_gt = jnp.sum(jnp.where(bits_of(aff_ref[...]) > thr, 1, 0), axis=1, keepdims=True)
    need = (cap - n_gt).astype(_F32)

    r = lax.broadcasted_iota(jnp.int32, (blk, blk), 0)
    c = lax.broadcasted_iota(jnp.int32, (blk, blk), 1)
    upper = jnp.where(r <= c, 1.0, 0.0).astype(_BF16)
    ones = jnp.ones((SUBLANES, blk), _BF16)

    group = SELECT_GROUP if n_blk % SELECT_GROUP == 0 else 1

    def blocks(jg, carry):
        eq_c, sel_c, sel_row = carry
        offs = [pl.multiple_of((jg * group + g) * blk, blk) for g in range(group)]
        bits = [bits_of(aff_ref[:, pl.ds(off, blk)]) for off in offs]
        eq_b = [jnp.where(b == thr, 1.0, 0.0).astype(_BF16) for b in bits]
        eq_local = _mm(jnp.concatenate(eq_b, axis=0), upper)
        sel_m, sel_b = [], []
        for g, b in enumerate(bits):
            eq_incl = eq_local[g * e:(g + 1) * e] + eq_c
            sel = (b > thr) | ((b == thr) & (eq_incl <= need))
            sel_m.append(sel)
            sel_b.append(jnp.where(sel, 1.0, 0.0).astype(_BF16))
            eq_c = eq_c + jnp.sum(eq_b[g].astype(_F32), axis=1, keepdims=True)
        sel_all = jnp.concatenate(sel_b, axis=0)
        sel_local = _mm(sel_all, upper)
        counts_row = _mm_nt(ones, sel_all)
        for g in range(group):
            sel_incl = sel_local[g * e:(g + 1) * e] + sel_c
            p_ref[:, pl.ds(offs[g], blk)] = jnp.where(sel_m[g], sel_incl, 0.0).astype(jnp.int32)
            starts_ref[pl.ds(jg * group + g, 1), :] = sel_row[0:1].astype(jnp.int32)
            sel_c = sel_c + jnp.sum(sel_b[g].astype(_F32), axis=1, keepdims=True)
            sel_row = sel_row + counts_row[:, g * e:(g + 1) * e]
        return eq_c, sel_c, sel_row

    zero = jnp.zeros((e, 1), _F32)
    _, _, totals = lax.fori_loop(0, n_blk // group, blocks, (zero, zero, jnp.zeros((SUBLANES, e), _F32)))
    starts_ref[n_blk:n_blk + 1, :] = totals[0:1].astype(jnp.int32)


def _select(aff, cap):
    e, t = aff.shape
    n_blk = t // SELECT_BLK
    return pl.pallas_call(
        functools.partial(_select_body, cap=cap, n_blk=n_blk),
        out_shape=[jax.ShapeDtypeStruct((e, t), jnp.int32), jax.ShapeDtypeStruct((n_blk + 1, e), jnp.int32)],
        compiler_params=pltpu.CompilerParams(vmem_limit_bytes=VMEM_LIMIT),
        name="expert_select",
    )(aff)


def _dispatch_body(starts_ref, hn_ref, p_ref, o_ref, *, cap, blocks_per_win):
    eb = pl.program_id(0)
    i = pl.program_id(1)
    tt = DISPATCH_TT
    stride = blocks_per_win * N_EXPERTS
    widths = (min(DISPATCH_WS, cap), min(DISPATCH_W, cap))

    @pl.when(i == 0)
    def _():
        o_ref[...] = jnp.zeros_like(o_ref)

    offs = {}
    all_short = None
    for s in range(DISPATCH_SUB):
        win = i * DISPATCH_SUB + s
        for j in range(DISPATCH_EP):
            e = eb * DISPATCH_EP + j
            start = starts_ref[win * stride + e]
            end = starts_ref[(win + 1) * stride + e]
            aligned = (start // BF16_ROWS) * BF16_ROWS
            offs[s, j] = [pl.multiple_of(jnp.minimum(aligned, cap - w), BF16_ROWS) for w in widths]
            short = end - offs[s, j][0] <= widths[0]
            all_short = short if all_short is None else jnp.logical_and(all_short, short)

    def gather(k):
        w = widths[k]
        slot = lax.broadcasted_iota(jnp.int32, (w, tt), 0)
        for s in range(DISPATCH_SUB):
            onehots = []
            for j in range(DISPATCH_EP):
                e = eb * DISPATCH_EP + j
                rel = p_ref[pl.ds(e, 1), s * tt:(s + 1) * tt] - 1 - offs[s, j][k]
                onehots.append(jnp.where(rel == slot, 1.0, 0.0).astype(_BF16))
            rows = _mm(jnp.concatenate(onehots, axis=0), hn_ref[s * tt:(s + 1) * tt, :]).astype(o_ref.dtype)
            for j in range(DISPATCH_EP):
                dst = (j, pl.ds(offs[s, j][k], w), slice(None))
                o_ref[dst] = o_ref[dst] + rows[j * w:(j + 1) * w]

    pl.when(all_short)(functools.partial(gather, 0))
    pl.when(jnp.logical_not(all_short))(functools.partial(gather, 1))


def _dispatch(starts, hn, p, cap):
    t, d = hn.shape
    tb = DISPATCH_TT * DISPATCH_SUB
    ep = DISPATCH_EP
    body = functools.partial(_dispatch_body, cap=cap, blocks_per_win=DISPATCH_TT // SELECT_BLK)
    return pl.pallas_call(
        body,
        grid_spec=pltpu.PrefetchScalarGridSpec(
            num_scalar_prefetch=1,
            grid=(N_EXPERTS // ep, t // tb),
            in_specs=[pl.BlockSpec((tb, d), lambda e, i, s: (i, 0)),
                      pl.BlockSpec((N_EXPERTS, tb), lambda e, i, s: (0, i))],
            out_specs=pl.BlockSpec((ep, cap, d), lambda e, i, s: (e, 0, 0))),
        out_shape=jax.ShapeDtypeStruct((N_EXPERTS, cap, d), _BF16),
        compiler_params=_params("parallel", "arbitrary"),
        name="expert_dispatch",
    )(starts, hn, p)


def _ffn_body(x_ref, wg_ref, wu_ref, wd_ref, o_ref, acc_ref, *, tm):
    fc = pl.program_id(1)

    @pl.when(fc == 0)
    def _():
        acc_ref[...] = jnp.zeros_like(acc_ref)

    wg = wg_ref[0, 0].astype(_BF16)
    wu = wu_ref[0, 0].astype(_BF16)
    wd = wd_ref[0, 0].astype(_BF16)
    for r in range(x_ref.shape[1] // tm):
        rows = slice(r * tm, (r + 1) * tm)
        x = x_ref[0, rows, :]
        a = _mm(x, wg)
        b = _mm(x, wu)
        hid = (a * _sigmoid(a) * b).astype(_BF16)
        acc_ref[rows, :] += _mm(hid, wd)

    @pl.when(fc == pl.num_programs(1) - 1)
    def _():
        o_ref[0] = acc_ref[...].astype(o_ref.dtype)


def _expert_ffn(xe, w_gate, w_up, w_down, layer):
    e, cap, d = xe.shape
    ff = w_gate.shape[3]
    fch = min(FFN_CHUNK, ff)
    tm = min(FFN_TM, cap)
    return pl.pallas_call(
        functools.partial(_ffn_body, tm=tm),
        grid=(e, ff // fch),
        in_specs=[pl.BlockSpec((1, cap, d), lambda e, c: (e, 0, 0)),
                  pl.BlockSpec((1, 1, d, fch), lambda e, c: (layer, e, 0, c)),
                  pl.BlockSpec((1, 1, d, fch), lambda e, c: (layer, e, 0, c)),
                  pl.BlockSpec((1, 1, fch, d), lambda e, c: (layer, e, c, 0))],
        out_specs=pl.BlockSpec((1, cap, d), lambda e, c: (e, 0, 0)),
        out_shape=jax.ShapeDtypeStruct((e, cap, d), _BF16),
        scratch_shapes=[pltpu.VMEM((cap, d), _F32)],
        compiler_params=_params("parallel", "arbitrary"),
        name="expert_ffn",
    )(xe, w_gate, w_up, w_down)


def _combine_body(starts_ref, xmid_ref, pt_ref, afft_ref, g_ref, ye_ref, o_ref, sbuf_ref, bbuf_ref, sem_ref,
                  *, cap, final_norm, ws, wl):
    i = pl.program_id(0)
    n = pl.num_programs(0)
    tt = COMBINE_TT
    stride = (tt // SELECT_BLK) * N_EXPERTS

    def windows(tile, slot):
        per_expert = []
        all_short = None
        for e in range(N_EXPERTS):
            start = starts_ref[tile * stride + e]
            end = starts_ref[(tile + 1) * stride + e]
            aligned = (start // BF16_ROWS) * BF16_ROWS
            pair = []
            for w, buf in ((ws, sbuf_ref), (wl, bbuf_ref)):
                a0 = pl.multiple_of(jnp.minimum(aligned, cap - w), BF16_ROWS)
                copy = pltpu.make_async_copy(ye_ref.at[e, pl.ds(a0, w), :], buf.at[slot, pl.ds(e * w, w), :],
                                             sem_ref.at[slot, e])
                pair.append((a0, copy))
            per_expert.append(pair)
            short = end - pair[0][0] <= ws
            all_short = short if all_short is None else jnp.logical_and(all_short, short)
        return all_short, per_expert

    def start_tile(tile, slot):
        all_short, per_expert = windows(tile, slot)
        for k in range(2):
            @pl.when(all_short if k == 0 else jnp.logical_not(all_short))
            def _():
                for pair in per_expert:
                    pair[k][1].start()

    @pl.when(i == 0)
    def _():
        start_tile(0, 0)

    @pl.when(i + 1 < n)
    def _():
        start_tile(i + 1, lax.rem(i + 1, 2))

    slot = lax.rem(i, 2)
    all_short, per_expert = windows(i, slot)

    def expand(k, w, buf):
        group = max(1, LANES // w)
        width = group * w
        lane = lax.broadcasted_iota(jnp.int32, (tt, width), 1)
        his, los = [], []
        for e0 in range(0, N_EXPERTS, group):
            rel = gate = None
            for j in range(group):
                e = e0 + j
                per_expert[e][k][1].wait()
                rel_e = pt_ref[:, e:e + 1] - 1 - per_expert[e][k][0]
                gate_e = afft_ref[:, e:e + 1]
                rel = rel_e if j == 0 else jnp.where(lane >= j * w, rel_e, rel)
                gate = gate_e if j == 0 else jnp.where(lane >= j * w, gate_e, gate)
            slot_of_lane = lane if group == 1 else lane & (w - 1)
            piece = jnp.where(rel == slot_of_lane, gate, 0.0)
            hi = piece.astype(_BF16)
            his.append(hi)
            los.append((piece - hi.astype(_F32)).astype(_BF16))
        lhs = jnp.concatenate([jnp.concatenate(his, axis=1), jnp.concatenate(los, axis=1)], axis=0)
        res = _mm(lhs, buf[slot])
        acc = xmid_ref[...] + res[:tt] + res[tt:]
        o_ref[...] = _rms(acc, g_ref[...]) if final_norm else acc

    pl.when(all_short)(functools.partial(expand, 0, ws, sbuf_ref))
    pl.when(jnp.logical_not(all_short))(functools.partial(expand, 1, wl, bbuf_ref))


def _combine(starts, xmid, p_t, aff_t, g_final, ye, cap, final_norm):
    t, d = xmid.shape
    tt = COMBINE_TT
    ws, wl = min(COMBINE_WS, cap), min(COMBINE_W, cap)
    return pl.pallas_call(
        functools.partial(_combine_body, cap=cap, final_norm=final_norm, ws=ws, wl=wl),
        grid_spec=pltpu.PrefetchScalarGridSpec(
            num_scalar_prefetch=1,
            grid=(t // tt,),
            in_specs=[pl.BlockSpec((tt, d), lambda i, s: (i, 0)),
                      pl.BlockSpec((tt, N_EXPERTS), lambda i, s: (i, 0)),
                      pl.BlockSpec((tt, N_EXPERTS), lambda i, s: (i, 0)),
                      pl.BlockSpec((1, d), lambda i, s: (0, 0)),
                      pl.BlockSpec(memory_space=pl.ANY)],
            out_specs=pl.BlockSpec((tt, d), lambda i, s: (i, 0)),
            scratch_shapes=[pltpu.VMEM((2, N_EXPERTS * ws, d), _BF16),
                            pltpu.VMEM((2, N_EXPERTS * wl, d), _BF16),
                            pltpu.SemaphoreType.DMA((2, N_EXPERTS))]),
        out_shape=jax.ShapeDtypeStruct((t, d), _F32),
        compiler_params=_params("arbitrary"),
        name="expert_combine",
    )(starts, xmid, p_t, aff_t, g_final, ye)


def _rope_tables(seq):
    half = ATT_HD // 2
    inv_freq = ROPE_THETA ** (-jnp.arange(half, dtype=_F32) / half)
    ang = jnp.arange(seq, dtype=_F32)[:, None] * inv_freq[None, :]
    cos, sin = jnp.cos(ang), jnp.sin(ang)
    return jnp.concatenate([cos, cos], axis=-1), jnp.concatenate([-sin, sin], axis=-1)


def _layer(x, batch, seq, w, final_g, final_norm):
    t, d = x.shape
    q, k, v, z, gates, aqkv, bg = _project(x, w, seq)
    g_t = bg[:, :4 * LA_HEADS].reshape(t // DELTA_CHUNK, DELTA_CHUNK, 4 * LA_HEADS).transpose(0, 2, 1)
    o_f, o_b = _delta_rule(q, k, v, bg, g_t, batch, seq)
    o_att = _window_attention(aqkv, w["sink"], batch, seq)
    xmid, hn, aff_t = _merge(o_f, o_b, z, gates, o_att, x, w["w_a"], w["w_b"], w["w_o"],
                             w["la_norm_g"], w["norm_ffn_g"], w["w_r"])

    cap = max(1, CAPACITY_FACTOR * t // N_EXPERTS)
    assert cap % LANES == 0 and cap % min(FFN_TM, cap) == 0
    p, starts = _select(aff_t.T, cap)
    starts = starts.reshape(-1)
    xe = _dispatch(starts, hn, p, cap)
    ye = _expert_ffn(xe, w["w_gate"], w["w_up"], w["w_down"], w["layer"])
    return _combine(starts, xmid, p.T, aff_t, final_g, ye, cap, final_norm)


def _layer_weights(l, seq, norm_mix_g, w_in, conv_w, la_a_log, la_dt_bias, la_norm_g, attn_sink, w_branch_a,
                   w_branch_b, w_out, norm_ffn_g, w_router, w_exp_gate, w_exp_up, w_exp_down):
    wi = w_in[l]
    c0 = CONV_CH
    c1 = c0 + LA_V
    c2 = c1 + 4 * LA_HEADS
    c3 = c2 + ATT_COLS
    pad = LANES - 4 * LA_HEADS
    zeros = jnp.zeros((2 * LA_HEADS,), _F32)
    lane_pad = lambda v: jnp.pad(jnp.concatenate([zeros, v.reshape(-1)]), (0, pad)).reshape(1, LANES)
    return {
        "norm_mix_g": norm_mix_g[l].reshape(1, -1),
        "w_conv": wi[:, :c0].astype(_BF16),
        "w_z": wi[:, c0:c1].astype(_BF16),
        "w_bg": jnp.pad(wi[:, c1:c2], ((0, 0), (0, pad))).astype(_BF16),
        "w_att": wi[:, c2:c3].astype(_BF16),
        "w_gates": wi[:, c3:].astype(_BF16),
        "a_log": lane_pad(la_a_log[l]),
        "dt_bias": lane_pad(la_dt_bias[l]),
        "rope": _rope_tables(seq),
        "conv_w": conv_w[l],
        "sink": attn_sink[l],
        "w_a": w_branch_a[l].astype(_BF16),
        "w_b": w_branch_b[l].astype(_BF16),
        "w_o": w_out[l].astype(_BF16),
        "la_norm_g": la_norm_g[l].reshape(1, -1),
        "norm_ffn_g": norm_ffn_g[l].reshape(1, -1),
        "w_r": w_router[l],
        "layer": l,
        "w_gate": w_exp_gate,
        "w_up": w_exp_up,
        "w_down": w_exp_down,
    }


def kernel(x_prompt, x_sample, norm_mix_g, w_in, conv_w, la_a_log, la_dt_bias, la_norm_g, attn_sink, w_branch_a, w_branch_b, w_out, norm_ffn_g, w_router, w_exp_gate, w_exp_up, w_exp_down, final_norm_g):
    depth = w_in.shape[0]
    layer_args = (norm_mix_g, w_in, conv_w, la_a_log, la_dt_bias, la_norm_g, attn_sink, w_branch_a, w_branch_b,
                  w_out, norm_ffn_g, w_router, w_exp_gate, w_exp_up, w_exp_down)
    final_g = final_norm_g.reshape(1, -1)

    def trunk(x):
        batch, seq, d = x.shape
        h = x.reshape(batch * seq, d)
        for l in range(depth):
            w = _layer_weights(l, seq, *layer_args)
            h = _layer(h, batch, seq, w, final_g, l == depth - 1)
        return h.reshape(batch, seq, d)

    return trunk(x_prompt), trunk(x_sample)
```

```python
import functools

import jax
import jax.numpy as jnp
from jax import lax
from jax.experimental import pallas as pl
from jax.experimental.pallas import tpu as pltpu

D_MODEL = 1024
LA_HEADS = 8
LA_DK = 128
LA_DV = 128
CONV_W = 4
ATT_Q_HEADS = 8
ATT_KV_HEADS = 2
ATT_HD = 128
WINDOW = 128
ROPE_THETA = 10000.0
N_EXPERTS = 16
EXPERT_FF = 2048
CAPACITY_FACTOR = 2
EPS = 1e-6

LA_QK = LA_HEADS * LA_DK
LA_V = LA_HEADS * LA_DV
ATT_Q = ATT_Q_HEADS * ATT_HD
ATT_KV = ATT_KV_HEADS * ATT_HD
CONV_CH = 2 * LA_QK + LA_V
ATT_COLS = ATT_Q + 2 * ATT_KV

LANES = 128
SUBLANES = 8
BF16_ROWS = 16
VMEM_LIMIT = 56 * 1024 * 1024

DELTA_CHUNK = 128
ATT_BLOCK = 128
ATT_QB = 16
ATT_MASKED = -1e30
PROJ_TM = 512
MERGE_TM = 512
DISPATCH_TT = 256
DISPATCH_SUB = 8
DISPATCH_EP = 4
DISPATCH_W = DISPATCH_TT + BF16_ROWS
DISPATCH_WS = 64
COMBINE_TT = 256
COMBINE_W = 384
COMBINE_WS = 64
FFN_TM = 1024
FFN_CHUNK = 512
SELECT_BLK = 128
SELECT_GROUP = 8

_BF16 = jnp.bfloat16
_F32 = jnp.float32


def _params(*sem):
    return pltpu.CompilerParams(dimension_semantics=sem, vmem_limit_bytes=VMEM_LIMIT)


def _rms(x, g):
    return x * lax.rsqrt(jnp.mean(x * x, axis=-1, keepdims=True) + EPS) * g


def _sigmoid(x):
    return 1.0 / (1.0 + jnp.exp(-x))


def _mm(a, b):
    return jnp.dot(a, b, preferred_element_type=_F32)


def _mm_nt(a, b):
    return lax.dot_general(a, b, (((1,), (1,)), ((), ())), preferred_element_type=_F32)


def _mm_tn(a, b):
    return lax.dot_general(a, b, (((0,), (0,)), ((), ())), preferred_element_type=_F32)


def _split3(x):
    h = x.astype(_BF16)
    r = x - h.astype(_F32)
    m = r.astype(_BF16)
    l = (r - m.astype(_F32)).astype(_BF16)
    return h, m, l


def _mm_exact_lhs(a01, x):
    h, m, l = _split3(x)
    return _mm(a01, h) + _mm(a01, m) + _mm(a01, l)


def _proj_body(x_ref, xp_ref, xn_ref, g_ref, wc_ref, wz_ref, wg_ref, wa_ref, wb_ref, cw_ref, alog_ref, dt_ref,
               cos_ref, sin_ref, q_ref, k_ref, v_ref, z_ref, gates_ref, aqkv_ref, bg_ref, hn_ref, ext_ref,
               *, tiles_per_seq, tm):
    i = pl.program_id(0)
    pos = lax.rem(i, tiles_per_seq)
    keep_prev = jnp.where(pos == 0, 0.0, 1.0)
    keep_next = jnp.where(pos == tiles_per_seq - 1, 0.0, 1.0)
    hr = BF16_ROWS
    g = g_ref[...]
    hn_ref[0:hr, :] = _rms(xp_ref[...], g).astype(_BF16)
    hn_ref[hr:hr + tm, :] = _rms(x_ref[...], g).astype(_BF16)
    hn_ref[hr + tm:, :] = _rms(xn_ref[...], g).astype(_BF16)
    hn = hn_ref[hr:hr + tm, :]

    for c, o_ref in enumerate((q_ref, k_ref, v_ref)):
        sl = slice(c * LA_QK, (c + 1) * LA_QK)
        raw = _mm(hn_ref[...], wc_ref[:, sl])
        ext_ref[0:hr, :] = raw[0:hr] * keep_prev
        ext_ref[hr:hr + tm, :] = raw[hr:hr + tm]
        ext_ref[hr + tm:, :] = raw[hr + tm:] * keep_next
        w = cw_ref[:, sl]
        ext = ext_ref[...]
        rows = ext.shape[0]
        taps = [pltpu.roll(ext, 2, 0), pltpu.roll(ext, 1, 0), ext, pltpu.roll(ext, rows - 1, 0)]
        y = sum(tap[hr:hr + tm] * w[j:j + 1] for j, tap in enumerate(taps))
        y = y * _sigmoid(y)
        if c == 2:
            o_ref[...] = y.astype(o_ref.dtype)
            continue
        scale = LA_DK ** -0.5 if c == 0 else 1.0
        for h in range(LA_HEADS):
            hs = slice(h * LA_DK, (h + 1) * LA_DK)
            yh = y[:, hs]
            yh = yh * lax.rsqrt(jnp.sum(yh * yh, axis=-1, keepdims=True) + EPS)
            o_ref[:, hs] = (yh * scale).astype(o_ref.dtype)

    z_ref[...] = _mm(hn, wz_ref[...]).astype(z_ref.dtype)
    for c in range(2):
        sl = slice(c * D_MODEL, (c + 1) * D_MODEL)
        gates_ref[:, sl] = _sigmoid(_mm(hn, wg_ref[:, sl])).astype(gates_ref.dtype)

    acc = _mm(hn, wa_ref[...])
    cos, sin = cos_ref[...], sin_ref[...]
    qcos, qsin = cos * (ATT_HD ** -0.5), sin * (ATT_HD ** -0.5)
    for h in range(ATT_Q_HEADS + ATT_KV_HEADS):
        sl = slice(h * ATT_HD, (h + 1) * ATT_HD)
        xh = acc[:, sl]
        c, s = (qcos, qsin) if h < ATT_Q_HEADS else (cos, sin)
        aqkv_ref[:, sl] = (xh * c + pltpu.roll(xh, ATT_HD // 2, 1) * s).astype(aqkv_ref.dtype)
    aqkv_ref[:, ATT_Q + ATT_KV:] = acc[:, ATT_Q + ATT_KV:].astype(aqkv_ref.dtype)

    acc = _mm(hn, wb_ref[...])
    zz = acc + dt_ref[...]
    sp = jnp.maximum(zz, 0.0) + jnp.log1p(jnp.exp(-jnp.abs(zz)))
    lane = lax.broadcasted_iota(jnp.int32, acc.shape, 1)
    decay = jnp.where(lane < 4 * LA_HEADS, -jnp.exp(alog_ref[...]) * sp, 0.0)
    bg_ref[...] = jnp.where(lane < 2 * LA_HEADS, _sigmoid(acc), decay)


def _project(x, w, seq):
    t, d = x.shape
    tm = min(PROJ_TM, seq)
    halo_blocks = tm // BF16_ROWS
    n_halo = t // BF16_ROWS
    tiles_per_seq = seq // tm
    row = lambda i: (i, 0)
    fix = lambda i: (0, 0)
    resident = lambda shape: pl.BlockSpec(shape, fix, pipeline_mode=pl.Buffered(1))
    rope_spec = pl.BlockSpec((tm, ATT_HD), lambda i: (lax.rem(i, tiles_per_seq), 0))
    bf = lambda n: jax.ShapeDtypeStruct((t, n), _BF16)
    return pl.pallas_call(
        functools.partial(_proj_body, tiles_per_seq=tiles_per_seq, tm=tm),
        grid=(t // tm,),
        in_specs=[pl.BlockSpec((tm, d), row),
                  pl.BlockSpec((BF16_ROWS, d), lambda i: (jnp.maximum(i * halo_blocks - 1, 0), 0)),
                  pl.BlockSpec((BF16_ROWS, d), lambda i: (jnp.minimum((i + 1) * halo_blocks, n_halo - 1), 0)),
                  resident((1, d)),
                  resident((d, CONV_CH)), resident((d, LA_V)), resident((d, 2 * D_MODEL)),
                  resident((d, ATT_COLS)), resident((d, LANES)),
                  resident((CONV_W, CONV_CH)), resident((1, LANES)), resident((1, LANES)),
                  rope_spec, rope_spec],
        out_specs=[pl.BlockSpec((tm, LA_QK), row), pl.BlockSpec((tm, LA_QK), row), pl.BlockSpec((tm, LA_V), row),
                   pl.BlockSpec((tm, LA_V), row), pl.BlockSpec((tm, 2 * D_MODEL), row),
                   pl.BlockSpec((tm, ATT_COLS), row), pl.BlockSpec((tm, LANES), row)],
        out_shape=[bf(LA_QK), bf(LA_QK), bf(LA_V), bf(LA_V), bf(2 * D_MODEL), bf(ATT_COLS),
                   jax.ShapeDtypeStruct((t, LANES), _F32)],
        scratch_shapes=[pltpu.VMEM((tm + 2 * BF16_ROWS, d), _BF16),
                        pltpu.VMEM((tm + 2 * BF16_ROWS, LA_QK), _F32)],
        compiler_params=_params("parallel"),
        name="in_proj",
    )(x, x, x, w["norm_mix_g"], w["w_conv"], w["w_z"], w["w_gates"], w["w_att"], w["w_bg"],
      w["conv_w"], w["a_log"], w["dt_bias"], *w["rope"])


def _delta_body(qf, kf, vf, bgf, gtf, qb, kb, vb, bgb, gtb, of_ref, ob_ref, state_ref):
    @pl.when(pl.program_id(1) == 0)
    def _():
        state_ref[...] = jnp.zeros_like(state_ref)

    c = DELTA_CHUNK
    row = lax.broadcasted_iota(jnp.int32, (c, c), 0)
    col = lax.broadcasted_iota(jnp.int32, (c, c), 1)
    eye = jnp.where(row == col, 1.0, 0.0)
    refs = ((qf, kf, vf, bgf, gtf, of_ref), (qb, kb, vb, bgb, gtb, ob_ref))

    chains = []
    for d, (q_ref, k_ref, v_ref, bg_ref, gt_ref, o_ref) in enumerate(refs):
        incl = (row >= col) if d == 0 else (row <= col)
        strict = (row > col) if d == 0 else (row < col)
        incl_b = jnp.where(incl, 1.0, 0.0).astype(_BF16)
        incl_t = jnp.where((row <= col) if d == 0 else (row >= col), 1.0, 0.0).astype(_BF16)
        bg = bg_ref[...]
        gc_cols = _mm_exact_lhs(incl_b, bg)
        h3 = _split3(gt_ref[0])
        gc_rows = _mm(h3[0], incl_t) + _mm(h3[1], incl_t) + _mm(h3[2], incl_t)
        total_row = c - 1 if d == 0 else 0
        for h in range(LA_HEADS):
            bcol = LA_HEADS * d + h
            gcol = 2 * LA_HEADS + LA_HEADS * d + h
            chains.append(dict(
                d=d, h=h, hs=slice(h * LA_DK, (h + 1) * LA_DK), incl=incl, strict=strict,
                q_ref=q_ref, k_ref=k_ref, v_ref=v_ref, o_ref=o_ref,
                beta=bg[:, bcol:bcol + 1], gc=gc_cols[:, gcol:gcol + 1], gc_r=gc_rows[gcol:gcol + 1, :],
                g_tot=gc_cols[total_row:total_row + 1, gcol:gcol + 1]))

    for ch in chains:
        q = ch["q_ref"][:, ch["hs"]].astype(_F32)
        k = ch["k_ref"][:, ch["hs"]]
        kf32 = k.astype(_F32)
        ch["kb"] = kf32 * ch["beta"]
        ch["egc"] = jnp.exp(ch["gc"])
        ch["a_qk"] = _mm_nt(jnp.concatenate([ch["kb"], q], axis=0).astype(_BF16), k)
        ch["qg"] = (q * ch["egc"]).astype(_BF16)
        ch["k_dec"] = (kf32 * jnp.exp(ch["g_tot"] - ch["gc"])).astype(_BF16)

    for ch in chains:
        incl = ch["incl"]
        decay = jnp.where(incl, jnp.exp(jnp.where(incl, ch["gc"] - ch["gc_r"], 0.0)), 0.0)
        l_mat = jnp.where(ch["strict"], ch["a_qk"][:c] * decay, 0.0)
        ch["qk"] = (ch["a_qk"][c:] * decay).astype(_BF16)
        ch["y"] = eye - l_mat
        l_b = l_mat.astype(_BF16)
        ch["m_b"] = _mm(l_b, l_b).astype(_BF16)
    p = 2
    while p < c:
        last = 2 * p >= c
        for ch in chains:
            lhs = ch["y"].astype(_BF16) if last else jnp.concatenate([ch["m_b"], ch["y"].astype(_BF16)], axis=0)
            prod = _mm(lhs, ch["m_b"])
            if last:
                ch["y"] = ch["y"] + prod
            else:
                ch["m_b"] = prod[:c].astype(_BF16)
                ch["y"] = ch["y"] + prod[c:]
        p *= 2

    for ch in chains:
        v = ch["v_ref"][:, ch["hs"]].astype(_F32)
        rhs = jnp.concatenate([v * ch["beta"], ch["kb"] * ch["egc"]], axis=1).astype(_BF16)
        ch["uw"] = _mm(ch["y"].astype(_BF16), rhs)
    for ch in chains:
        ch["state"] = state_ref[ch["d"], ch["h"]]
        w_b = ch["uw"][:, LA_DV:].astype(_BF16)
        ch["ws_qs"] = _mm(jnp.concatenate([w_b, ch["qg"]], axis=0), ch["state"].astype(_BF16))
    for ch in chains:
        v_new_b = (ch["uw"][:, :LA_DV] - ch["ws_qs"][:c]).astype(_BF16)
        ch["o_ref"][:, ch["hs"]] = ch["ws_qs"][c:] + _mm(ch["qk"], v_new_b)
        state_ref[ch["d"], ch["h"]] = ch["state"] * jnp.exp(ch["g_tot"]) + _mm_tn(ch["k_dec"], v_new_b)


def _delta_rule(q, k, v, bg, g_t, batch, seq):
    t = q.shape[0]
    c = DELTA_CHUNK
    n = seq // c
    fwd = lambda b, j: (b * n + j, 0)
    bwd = lambda b, j: (b * n + n - 1 - j, 0)
    fwd3 = lambda b, j: (b * n + j, 0, 0)
    bwd3 = lambda b, j: (b * n + n - 1 - j, 0, 0)

    def specs(im, im3):
        return [pl.BlockSpec((c, LA_QK), im), pl.BlockSpec((c, LA_QK), im), pl.BlockSpec((c, LA_V), im),
                pl.BlockSpec((c, LANES), im), pl.BlockSpec((1, 4 * LA_HEADS, c), im3)]

    out = jax.ShapeDtypeStruct((t, LA_V), _F32)
    return pl.pallas_call(
        _delta_body,
        grid=(batch, n),
        in_specs=specs(fwd, fwd3) + specs(bwd, bwd3),
        out_specs=[pl.BlockSpec((c, LA_V), fwd), pl.BlockSpec((c, LA_V), bwd)],
        out_shape=[out, out],
        scratch_shapes=[pltpu.VMEM((2, LA_HEADS, LA_DK, LA_DV), _F32)],
        compiler_params=_params("parallel", "arbitrary"),
        name="delta_rule",
    )(q, k, v, bg, g_t, q, k, v, bg, g_t)


def _attn_body(sink_ref, q_ref, kp_ref, kc_ref, kn_ref, vp_ref, vc_ref, vn_ref, o_ref, *, seq, nq):
    n = pl.program_id(1)
    blk = ATT_BLOCK
    group = ATT_Q_HEADS // ATT_KV_HEADS
    kw = jnp.concatenate([kp_ref[...], kc_ref[...], kn_ref[...]], axis=0)
    vw = jnp.concatenate([vp_ref[...], vc_ref[...], vn_ref[...]], axis=0)
    row = lax.broadcasted_iota(jnp.int32, (blk, 3 * blk), 0)
    col = lax.broadcasted_iota(jnp.int32, (blk, 3 * blk), 1)
    band = jnp.abs(row + blk - col) <= WINDOW

    chains = []
    for qb in range(nq):
        kpos = (nq * n + qb - 1) * blk + col
        bias = jnp.where(band & (kpos >= 0) & (kpos < seq), 0.0, ATT_MASKED)
        bias = jnp.concatenate([bias] * group, axis=0)
        for g in range(ATT_KV_HEADS):
            heads = [g * group + j for j in range(group)]
            chains.append(dict(qb=qb, g=g, heads=heads, bias=bias, rows=slice(qb * blk, (qb + 1) * blk),
                               keys=slice(qb * blk, (qb + 3) * blk), cols=slice(g * ATT_HD, (g + 1) * ATT_HD)))
    for ch in chains:
        qs = jnp.concatenate([q_ref[ch["rows"], h * ATT_HD:(h + 1) * ATT_HD] for h in ch["heads"]], axis=0)
        ch["s"] = _mm_nt(qs, kw[ch["keys"], ch["cols"]]) + ch["bias"]
    for ch in chains:
        s = ch["s"]
        sk = jnp.concatenate([jnp.full((blk, 1), sink_ref[h], _F32) for h in ch["heads"]], axis=0)
        m = jnp.maximum(jnp.max(s, axis=-1, keepdims=True), sk)
        p = jnp.exp(s - m)
        ch["denom"] = jnp.sum(p, axis=-1, keepdims=True) + jnp.exp(sk - m)
        ch["p"] = p.astype(_BF16)
    for ch in chains:
        o = _mm(ch["p"], vw[ch["keys"], ch["cols"]]) / ch["denom"]
        for j, h in enumerate(ch["heads"]):
            o_ref[ch["rows"], h * ATT_HD:(h + 1) * ATT_HD] = o[j * blk:(j + 1) * blk].astype(o_ref.dtype)


def _window_attention(aqkv, sink, batch, seq):
    t = aqkv.shape[0]
    blk = ATT_BLOCK
    nb = seq // blk
    nq = ATT_QB if nb % ATT_QB == 0 else 1
    ns = nb // nq
    kcol = ATT_Q // ATT_KV
    prev = lambda b, n: b * nb + jnp.maximum(nq * n - 1, 0)
    nxt = lambda b, n: b * nb + jnp.minimum(nq * n + nq, nb - 1)
    cur = lambda b, n: b * ns + n
    kv_specs = []
    for c in (kcol, kcol + 1):
        kv_specs += [pl.BlockSpec((blk, ATT_KV), functools.partial(lambda b, n, c: (prev(b, n), c), c=c)),
                     pl.BlockSpec((nq * blk, ATT_KV), functools.partial(lambda b, n, c: (cur(b, n), c), c=c)),
                     pl.BlockSpec((blk, ATT_KV), functools.partial(lambda b, n, c: (nxt(b, n), c), c=c))]
    return pl.pallas_call(
        functools.partial(_attn_body, seq=seq, nq=nq),
        grid=(batch, ns),
        in_specs=[pl.BlockSpec(memory_space=pltpu.SMEM),
                  pl.BlockSpec((nq * blk, ATT_Q), lambda b, n: (cur(b, n), 0)),
                  *kv_specs],
        out_specs=pl.BlockSpec((nq * blk, ATT_Q), lambda b, n: (cur(b, n), 0)),
        out_shape=jax.ShapeDtypeStruct((t, ATT_Q), _BF16),
        compiler_params=_params("parallel", "parallel"),
        name="window_attention",
    )(sink, *([aqkv] * 7))


def _merge_body(of_ref, ob_ref, z_ref, ga_ref, gb_ref, oatt_ref, x_ref, wa_ref, wb_ref, wo_ref,
                ng_ref, fg_ref, wr_ref, xmid_ref, hn_ref, aff_ref, oa_ref):
    o = of_ref[...] + ob_ref[...]
    z = z_ref[...].astype(_F32)
    gate = z * _sigmoid(z)
    for h in range(LA_HEADS):
        hs = slice(h * LA_DV, (h + 1) * LA_DV)
        oh = o[:, hs]
        oh = _rms(oh, ng_ref[...])
        oa_ref[:, hs] = (oh * gate[:, hs]).astype(_BF16)
    merged = (ga_ref[...].astype(_F32) * _mm(oa_ref[...], wa_ref[...])
              + gb_ref[...].astype(_F32) * _mm(oatt_ref[...], wb_ref[...]))
    xmid = x_ref[...] + _mm(merged.astype(_BF16), wo_ref[...])
    xmid_ref[...] = xmid
    hn = _rms(xmid, fg_ref[...])
    hn_ref[...] = hn.astype(_BF16)
    tm = hn.shape[0]
    wh, wl, _ = _split3(wr_ref[...])
    hh, hl, _ = _split3(hn)
    r1 = _mm(jnp.concatenate([hh, hl], axis=0), wh)
    logits = r1[:tm] + r1[tm:] + _mm(hh, wl)
    e = jnp.exp(logits - jnp.max(logits, axis=-1, keepdims=True))
    aff_ref[...] = e / jnp.sum(e, axis=-1, keepdims=True)


def _merge(o_f, o_b, z, gates, o_att, x, w_a, w_b, w_o, norm_g, ffn_g, w_r):
    t, d = x.shape
    tm = min(MERGE_TM, t)
    row = lambda i: (i, 0)
    fix = lambda i: (0, 0)
    act = pl.BlockSpec((tm, d), row)
    wspec = pl.BlockSpec((d, d), fix)
    return pl.pallas_call(
        _merge_body,
        grid=(t // tm,),
        in_specs=[act, act, act, act, pl.BlockSpec((tm, d), lambda i: (i, 1)), act, act,
                  wspec, wspec, wspec,
                  pl.BlockSpec((1, LA_DV), fix), pl.BlockSpec((1, d), fix),
                  pl.BlockSpec((d, N_EXPERTS), fix)],
        out_specs=[act, act, pl.BlockSpec((tm, N_EXPERTS), row)],
        out_shape=[jax.ShapeDtypeStruct((t, d), _F32), jax.ShapeDtypeStruct((t, d), _BF16),
                   jax.ShapeDtypeStruct((t, N_EXPERTS), _F32)],
        scratch_shapes=[pltpu.VMEM((tm, d), _BF16)],
        compiler_params=_params("parallel"),
        name="merge_router",
    )(o_f, o_b, z, gates, gates, o_att, x, w_a, w_b, w_o, norm_g, ffn_g, w_r)


def _select_body(aff_ref, p_ref, starts_ref, *, cap, n_blk):
    e = N_EXPERTS
    blk = SELECT_BLK

    def bits_of(x):
        return pltpu.bitcast(x, jnp.int32)

    def bisect(it, prefix):
        cand = prefix | jnp.left_shift(jnp.int32(1), 30 - it)
        cnt = jnp.sum(jnp.where(bits_of(aff_ref[...]) >= cand, 1, 0), axis=1, keepdims=True)
        return jnp.where(cnt >= cap, cand, prefix)

    thr = lax.fori_loop(0, 31, bisect, jnp.zeros((e, 1), jnp.int32))
    n_gt = jnp.sum(jnp.where(bits_of(aff_ref[...]) > thr, 1, 0), axis=1, keepdims=True)
    need = (cap - n_gt).astype(_F32)

    r = lax.broadcasted_iota(jnp.int32, (blk, blk), 0)
    c = lax.broadcasted_iota(jnp.int32, (blk, blk), 1)
    upper = jnp.where(r <= c, 1.0, 0.0).astype(_BF16)
    ones = jnp.ones((SUBLANES, blk), _BF16)

    group = SELECT_GROUP if n_blk % SELECT_GROUP == 0 else 1

    def blocks(jg, carry):
        eq_c, sel_c, sel_row = carry
        offs = [pl.multiple_of((jg * group + g) * blk, blk) for g in range(group)]
        bits = [bits_of(aff_ref[:, pl.ds(off, blk)]) for off in offs]
        eq_b = [jnp.where(b == thr, 1.0, 0.0).astype(_BF16) for b in bits]
        eq_local = _mm(jnp.concatenate(eq_b, axis=0), upper)
        sel_m, sel_b = [], []
        for g, b in enumerate(bits):
            eq_incl = eq_local[g * e:(g + 1) * e] + eq_c
            sel = (b > thr) | ((b == thr) & (eq_incl <= need))
            sel_m.append(sel)
            sel_b.append(jnp.where(sel, 1.0, 0.0).astype(_BF16))
            eq_c = eq_c + jnp.sum(eq_b[g].astype(_F32), axis=1, keepdims=True)
        sel_all = jnp.concatenate(sel_b, axis=0)
        sel_local = _mm(sel_all, upper)
        counts_row = _mm_nt(ones, sel_all)
        for g in range(group):
            sel_incl = sel_local[g * e:(g + 1) * e] + sel_c
            p_ref[:, pl.ds(offs[g], blk)] = jnp.where(sel_m[g], sel_incl, 0.0).astype(jnp.int32)
            starts_ref[pl.ds(jg * group + g, 1), :] = sel_row[0:1].astype(jnp.int32)
            sel_c = sel_c + jnp.sum(sel_b[g].astype(_F32), axis=1, keepdims=True)
            sel_row = sel_row + counts_row[:, g * e:(g + 1) * e]
        return eq_c, sel_c, sel_row

    zero = jnp.zeros((e, 1), _F32)
    _, _, totals = lax.fori_loop(0, n_blk // group, blocks, (zero, zero, jnp.zeros((SUBLANES, e), _F32)))
    starts_ref[n_blk:n_blk + 1, :] = totals[0:1].astype(jnp.int32)


def _select(aff, cap):
    e, t = aff.shape
    n_blk = t // SELECT_BLK
    return pl.pallas_call(
        functools.partial(_select_body, cap=cap, n_blk=n_blk),
        out_shape=[jax.ShapeDtypeStruct((e, t), jnp.int32), jax.ShapeDtypeStruct((n_blk + 1, e), jnp.int32)],
        compiler_params=pltpu.CompilerParams(vmem_limit_bytes=VMEM_LIMIT),
        name="expert_select",
    )(aff)


def _dispatch_body(starts_ref, hn_ref, p_ref, o_ref, *, cap, blocks_per_win):
    eb = pl.program_id(0)
    i = pl.program_id(1)
    tt = DISPATCH_TT
    stride = blocks_per_win * N_EXPERTS
    widths = (min(DISPATCH_WS, cap), min(DISPATCH_W, cap))

    @pl.when(i == 0)
    def _():
        o_ref[...] = jnp.zeros_like(o_ref)

    offs = {}
    all_short = None
    for s in range(DISPATCH_SUB):
        win = i * DISPATCH_SUB + s
        for j in range(DISPATCH_EP):
            e = eb * DISPATCH_EP + j
            start = starts_ref[win * stride + e]
            end = starts_ref[(win + 1) * stride + e]
            aligned = (start // BF16_ROWS) * BF16_ROWS
            offs[s, j] = [pl.multiple_of(jnp.minimum(aligned, cap - w), BF16_ROWS) for w in widths]
            short = end - offs[s, j][0] <= widths[0]
            all_short = short if all_short is None else jnp.logical_and(all_short, short)

    def gather(k):
        w = widths[k]
        slot = lax.broadcasted_iota(jnp.int32, (w, tt), 0)
        for s in range(DISPATCH_SUB):
            onehots = []
            for j in range(DISPATCH_EP):
                e = eb * DISPATCH_EP + j
                rel = p_ref[pl.ds(e, 1), s * tt:(s + 1) * tt] - 1 - offs[s, j][k]
                onehots.append(jnp.where(rel == slot, 1.0, 0.0).astype(_BF16))
            rows = _mm(jnp.concatenate(onehots, axis=0), hn_ref[s * tt:(s + 1) * tt, :]).astype(o_ref.dtype)
            for j in range(DISPATCH_EP):
                dst = (j, pl.ds(offs[s, j][k], w), slice(None))
                o_ref[dst] = o_ref[dst] + rows[j * w:(j + 1) * w]

    pl.when(all_short)(functools.partial(gather, 0))
    pl.when(jnp.logical_not(all_short))(functools.partial(gather, 1))


def _dispatch(starts, hn, p, cap):
    t, d = hn.shape
    tb = DISPATCH_TT * DISPATCH_SUB
    ep = DISPATCH_EP
    body = functools.partial(_dispatch_body, cap=cap, blocks_per_win=DISPATCH_TT // SELECT_BLK)
    return pl.pallas_call(
        body,
        grid_spec=pltpu.PrefetchScalarGridSpec(
            num_scalar_prefetch=1,
            grid=(N_EXPERTS // ep, t // tb),
            in_specs=[pl.BlockSpec((tb, d), lambda e, i, s: (i, 0)),
                      pl.BlockSpec((N_EXPERTS, tb), lambda e, i, s: (0, i))],
            out_specs=pl.BlockSpec((ep, cap, d), lambda e, i, s: (e, 0, 0))),
        out_shape=jax.ShapeDtypeStruct((N_EXPERTS, cap, d), _BF16),
        compiler_params=_params("parallel", "arbitrary"),
        name="expert_dispatch",
    )(starts, hn, p)


def _ffn_body(x_ref, wg_ref, wu_ref, wd_ref, o_ref, acc_ref, *, tm):
    fc = pl.program_id(1)

    @pl.when(fc == 0)
    def _():
        acc_ref[...] = jnp.zeros_like(acc_ref)

    wg = wg_ref[0, 0].astype(_BF16)
    wu = wu_ref[0, 0].astype(_BF16)
    wd = wd_ref[0, 0].astype(_BF16)
    for r in range(x_ref.shape[1] // tm):
        rows = slice(r * tm, (r + 1) * tm)
        x = x_ref[0, rows, :]
        a = _mm(x, wg)
        b = _mm(x, wu)
        hid = (a * _sigmoid(a) * b).astype(_BF16)
        acc_ref[rows, :] += _mm(hid, wd)

    @pl.when(fc == pl.num_programs(1) - 1)
    def _():
        o_ref[0] = acc_ref[...].astype(o_ref.dtype)


def _expert_ffn(xe, w_gate, w_up, w_down, layer):
    e, cap, d = xe.shape
    ff = w_gate.shape[3]
    fch = min(FFN_CHUNK, ff)
    tm = min(FFN_TM, cap)
    return pl.pallas_call(
        functools.partial(_ffn_body, tm=tm),
        grid=(e, ff // fch),
        in_specs=[pl.BlockSpec((1, cap, d), lambda e, c: (e, 0, 0)),
                  pl.BlockSpec((1, 1, d, fch), lambda e, c: (layer, e, 0, c)),
                  pl.BlockSpec((1, 1, d, fch), lambda e, c: (layer, e, 0, c)),
                  pl.BlockSpec((1, 1, fch, d), lambda e, c: (layer, e, c, 0))],
        out_specs=pl.BlockSpec((1, cap, d), lambda e, c: (e, 0, 0)),
        out_shape=jax.ShapeDtypeStruct((e, cap, d), _BF16),
        scratch_shapes=[pltpu.VMEM((cap, d), _F32)],
        compiler_params=_params("parallel", "arbitrary"),
        name="expert_ffn",
    )(xe, w_gate, w_up, w_down)


def _combine_body(starts_ref, xmid_ref, pt_ref, afft_ref, g_ref, ye_ref, o_ref, sbuf_ref, bbuf_ref, sem_ref,
                  *, cap, final_norm, ws, wl):
    i = pl.program_id(0)
    n = pl.num_programs(0)
    tt = COMBINE_TT
    stride = (tt // SELECT_BLK) * N_EXPERTS

    def windows(tile, slot):
        per_expert = []
        all_short = None
        for e in range(N_EXPERTS):
            start = starts_ref[tile * stride + e]
            end = starts_ref[(tile + 1) * stride + e]
            aligned = (start // BF16_ROWS) * BF16_ROWS
            pair = []
            for w, buf in ((ws, sbuf_ref), (wl, bbuf_ref)):
                a0 = pl.multiple_of(jnp.minimum(aligned, cap - w), BF16_ROWS)
                copy = pltpu.make_async_copy(ye_ref.at[e, pl.ds(a0, w), :], buf.at[slot, pl.ds(e * w, w), :],
                                             sem_ref.at[slot, e])
                pair.append((a0, copy))
            per_expert.append(pair)
            short = end - pair[0][0] <= ws
            all_short = short if all_short is None else jnp.logical_and(all_short, short)
        return all_short, per_expert

    def start_tile(tile, slot):
        all_short, per_expert = windows(tile, slot)
        for k in range(2):
            @pl.when(all_short if k == 0 else jnp.logical_not(all_short))
            def _():
                for pair in per_expert:
                    pair[k][1].start()

    @pl.when(i == 0)
    def _():
        start_tile(0, 0)

    @pl.when(i + 1 < n)
    def _():
        start_tile(i + 1, lax.rem(i + 1, 2))

    slot = lax.rem(i, 2)
    all_short, per_expert = windows(i, slot)

    def expand(k, w, buf):
        group = max(1, LANES // w)
        width = group * w
        lane = lax.broadcasted_iota(jnp.int32, (tt, width), 1)
        his, los = [], []
        for e0 in range(0, N_EXPERTS, group):
            rel = gate = None
            for j in range(group):
                e = e0 + j
                per_expert[e][k][1].wait()
                rel_e = pt_ref[:, e:e + 1] - 1 - per_expert[e][k][0]
                gate_e = afft_ref[:, e:e + 1]
                rel = rel_e if j == 0 else jnp.where(lane >= j * w, rel_e, rel)
                gate = gate_e if j == 0 else jnp.where(lane >= j * w, gate_e, gate)
            slot_of_lane = lane if group == 1 else lane & (w - 1)
            piece = jnp.where(rel == slot_of_lane, gate, 0.0)
            hi = piece.astype(_BF16)
            his.append(hi)
            los.append((piece - hi.astype(_F32)).astype(_BF16))
        lhs = jnp.concatenate([jnp.concatenate(his, axis=1), jnp.concatenate(los, axis=1)], axis=0)
        res = _mm(lhs, buf[slot])
        acc = xmid_ref[...] + res[:tt] + res[tt:]
        o_ref[...] = _rms(acc, g_ref[...]) if final_norm else acc

    pl.when(all_short)(functools.partial(expand, 0, ws, sbuf_ref))
    pl.when(jnp.logical_not(all_short))(functools.partial(expand, 1, wl, bbuf_ref))


def _combine(starts, xmid, p_t, aff_t, g_final, ye, cap, final_norm):
    t, d = xmid.shape
    tt = COMBINE_TT
    ws, wl = min(COMBINE_WS, cap), min(COMBINE_W, cap)
    return pl.pallas_call(
        functools.partial(_combine_body, cap=cap, final_norm=final_norm, ws=ws, wl=wl),
        grid_spec=pltpu.PrefetchScalarGridSpec(
            num_scalar_prefetch=1,
            grid=(t // tt,),
            in_specs=[pl.BlockSpec((tt, d), lambda i, s: (i, 0)),
                      pl.BlockSpec((tt, N_EXPERTS), lambda i, s: (i, 0)),
                      pl.BlockSpec((tt, N_EXPERTS), lambda i, s: (i, 0)),
                      pl.BlockSpec((1, d), lambda i, s: (0, 0)),
                      pl.BlockSpec(memory_space=pl.ANY)],
            out_specs=pl.BlockSpec((tt, d), lambda i, s: (i, 0)),
            scratch_shapes=[pltpu.VMEM((2, N_EXPERTS * ws, d), _BF16),
                            pltpu.VMEM((2, N_EXPERTS * wl, d), _BF16),
                            pltpu.SemaphoreType.DMA((2, N_EXPERTS))]),
        out_shape=jax.ShapeDtypeStruct((t, d), _F32),
        compiler_params=_params("arbitrary"),
        name="expert_combine",
    )(starts, xmid, p_t, aff_t, g_final, ye)


def _rope_tables(seq):
    half = ATT_HD // 2
    inv_freq = ROPE_THETA ** (-jnp.arange(half, dtype=_F32) / half)
    ang = jnp.arange(seq, dtype=_F32)[:, None] * inv_freq[None, :]
    cos, sin = jnp.cos(ang), jnp.sin(ang)
    return jnp.concatenate([cos, cos], axis=-1), jnp.concatenate([-sin, sin], axis=-1)


def _layer(x, batch, seq, w, final_g, final_norm):
    t, d = x.shape
    q, k, v, z, gates, aqkv, bg = _project(x, w, seq)
    g_t = bg[:, :4 * LA_HEADS].reshape(t // DELTA_CHUNK, DELTA_CHUNK, 4 * LA_HEADS).transpose(0, 2, 1)
    o_f, o_b = _delta_rule(q, k, v, bg, g_t, batch, seq)
    o_att = _window_attention(aqkv, w["sink"], batch, seq)
    xmid, hn, aff_t = _merge(o_f, o_b, z, gates, o_att, x, w["w_a"], w["w_b"], w["w_o"],
                             w["la_norm_g"], w["norm_ffn_g"], w["w_r"])

    cap = max(1, CAPACITY_FACTOR * t // N_EXPERTS)
    assert cap % LANES == 0 and cap % min(FFN_TM, cap) == 0
    p, starts = _select(aff_t.T, cap)
    starts = starts.reshape(-1)
    xe = _dispatch(starts, hn, p, cap)
    ye = _expert_ffn(xe, w["w_gate"], w["w_up"], w["w_down"], w["layer"])
    return _combine(starts, xmid, p.T, aff_t, final_g, ye, cap, final_norm)


def _layer_weights(l, seq, norm_mix_g, w_in, conv_w, la_a_log, la_dt_bias, la_norm_g, attn_sink, w_branch_a,
                   w_branch_b, w_out, norm_ffn_g, w_router, w_exp_gate, w_exp_up, w_exp_down):
    wi = w_in[l]
    c0 = CONV_CH
    c1 = c0 + LA_V
    c2 = c1 + 4 * LA_HEADS
    c3 = c2 + ATT_COLS
    pad = LANES - 4 * LA_HEADS
    zeros = jnp.zeros((2 * LA_HEADS,), _F32)
    lane_pad = lambda v: jnp.pad(jnp.concatenate([zeros, v.reshape(-1)]), (0, pad)).reshape(1, LANES)
    return {
        "norm_mix_g": norm_mix_g[l].reshape(1, -1),
        "w_conv": wi[:, :c0].astype(_BF16),
        "w_z": wi[:, c0:c1].astype(_BF16),
        "w_bg": jnp.pad(wi[:, c1:c2], ((0, 0), (0, pad))).astype(_BF16),
        "w_att": wi[:, c2:c3].astype(_BF16),
        "w_gates": wi[:, c3:].astype(_BF16),
        "a_log": lane_pad(la_a_log[l]),
        "dt_bias": lane_pad(la_dt_bias[l]),
        "rope": _rope_tables(seq),
        "conv_w": conv_w[l],
        "sink": attn_sink[l],
        "w_a": w_branch_a[l].astype(_BF16),
        "w_b": w_branch_b[l].astype(_BF16),
        "w_o": w_out[l].astype(_BF16),
        "la_norm_g": la_norm_g[l].reshape(1, -1),
        "norm_ffn_g": norm_ffn_g[l].reshape(1, -1),
        "w_r": w_router[l],
        "layer": l,
        "w_gate": w_exp_gate,
        "w_up": w_exp_up,
        "w_down": w_exp_down,
    }


def kernel(x_prompt, x_sample, norm_mix_g, w_in, conv_w, la_a_log, la_dt_bias, la_norm_g, attn_sink, w_branch_a, w_branch_b, w_out, norm_ffn_g, w_router, w_exp_gate, w_exp_up, w_exp_down, final_norm_g):
    depth = w_in.shape[0]
    layer_args = (norm_mix_g, w_in, conv_w, la_a_log, la_dt_bias, la_norm_g, attn_sink, w_branch_a, w_branch_b,
                  w_out, norm_ffn_g, w_router, w_exp_gate, w_exp_up, w_exp_down)
    final_g = final_norm_g.reshape(1, -1)

    def trunk(x):
        batch, seq, d = x.shape
        h = x.reshape(batch * seq, d)
        for l in range(depth):
            w = _layer_weights(l, seq, *layer_args)
            h = _layer(h, batch, seq, w, final_g, l == depth - 1)
        return h.reshape(batch, seq, d)

    return trunk(x_prompt), trunk(x_sample)
```
